```python
import jax, jax.numpy as jnp
from jax import lax
import numpy as np

D_MODEL = 1024
BATCH = 1
SEQ = 16384
DEPTH = 4

HEAD_DIM = 64
A_HEADS = 8
B_HEADS = 4
C_HEADS = 4
A_WIDTH = A_HEADS * HEAD_DIM
B_WIDTH = B_HEADS * HEAD_DIM
C_WIDTH = C_HEADS * HEAD_DIM
MIX_WIDTH = A_WIDTH + B_WIDTH + C_WIDTH
MOBA_BLOCK = 256
MOBA_TOPK = 3
MOBA_QCHUNK = 128
RET_CHUNK = 256
MLSTM_CHUNK = 128
PAD_MULT = 256
CONV_K = 4
RMS_EPS = 1e-6
HEAD_NORM_EPS = 1e-5
ROPE_BASE = 10000.0
SPLIT_SIZES = [A_WIDTH] * 4 + [B_WIDTH] * 4 + [C_WIDTH] * 5 + [C_HEADS, C_HEADS]
N_IN = sum(SPLIT_SIZES)

kernel_name = "hymba_style_moba_retention_mlstm_trunk"


def rmsnorm(x, w):
    x32 = x.astype(jnp.float32)
    y = x32 * lax.rsqrt(jnp.mean(x32 * x32, axis=-1, keepdims=True) + RMS_EPS)
    return (y * w.astype(jnp.float32)).astype(x.dtype)


def split_heads(t, n_heads):
    b, s, _ = t.shape
    return t.reshape(b, s, n_heads, HEAD_DIM).transpose(0, 2, 1, 3)


def merge_heads(t):
    b, h, s, d = t.shape
    return t.transpose(0, 2, 1, 3).reshape(b, s, h * d)


def head_norm(y, w):
    mu = jnp.mean(y, axis=-1, keepdims=True)
    var = jnp.mean(jnp.square(y - mu), axis=-1, keepdims=True)
    y = (y - mu) * lax.rsqrt(var + HEAD_NORM_EPS)
    return merge_heads(y) * w.astype(jnp.float32)


def alibi_slopes(n_heads):
    return jnp.exp2(-8.0 * jnp.arange(1, n_heads + 1, dtype=jnp.float32) / n_heads)


def rotate(t, pos):
    half = t.shape[-1] // 2
    theta = 1.0 / (ROPE_BASE ** jnp.linspace(0.0, 1.0, half, dtype=jnp.float32))
    ang = pos[:, None] * theta[None, :]
    cos, sin = jnp.cos(ang), jnp.sin(ang)
    t1, t2 = t[..., :half], t[..., half:]
    return jnp.concatenate([t1 * cos - t2 * sin, t1 * sin + t2 * cos], axis=-1)


def moba_attention(q, k, v):
    b, h, s, dh = q.shape
    nb = s // MOBA_BLOCK
    n_sel = min(MOBA_TOPK, nb - 1)
    scale = dh ** -0.5
    slopes = alibi_slopes(h)[None, :, None, None]
    kb = k.reshape(b, h, nb, MOBA_BLOCK, dh)
    vb = v.reshape(b, h, nb, MOBA_BLOCK, dh)
    kmean = jnp.mean(kb, axis=3)
    bi = jnp.arange(b)[:, None, None, None]
    hi = jnp.arange(h)[None, :, None, None]
    offs = jnp.arange(MOBA_BLOCK)
    qloc = jnp.arange(MOBA_QCHUNK)

    def one_chunk(c):
        t0 = c * MOBA_QCHUNK
        qc = lax.dynamic_slice_in_dim(q, t0, MOBA_QCHUNK, axis=2)
        tpos = t0 + qloc
        j = t0 // MOBA_BLOCK
        k_own = lax.dynamic_slice_in_dim(k, j * MOBA_BLOCK, MOBA_BLOCK, axis=2)
        v_own = lax.dynamic_slice_in_dim(v, j * MOBA_BLOCK, MOBA_BLOCK, axis=2)
        dist_own = tpos[:, None] - (j * MOBA_BLOCK + offs)[None, :]
        s_own = jnp.einsum('bhqd,bhkd->bhqk', qc, k_own) * scale - slopes * dist_own
        s_own = jnp.where(dist_own >= 0, s_own, -jnp.inf)
        if n_sel == 0:
            p = jax.nn.softmax(s_own, axis=-1)
            return jnp.einsum('bhqk,bhkd->bhqd', p, v_own)
        gate = jnp.einsum('bhqd,bhnd->bhqn', qc, kmean)
        gate = jnp.where(jnp.arange(nb) < j, gate, -jnp.inf)
        _, idx = lax.top_k(gate, n_sel)
        valid = jnp.arange(n_sel) < j
        k_sel = kb[bi, hi, idx]
        v_sel = vb[bi, hi, idx]
        dist = tpos[:, None, None] - (idx[..., None] * MOBA_BLOCK + offs)
        s_sel = jnp.einsum('bhqd,bhqnkd->bhqnk', qc, k_sel) * scale - slopes[..., None] * dist
        s_sel = jnp.where(valid[:, None], s_sel, -jnp.inf)
        n_k = n_sel * MOBA_BLOCK
        s_all = jnp.concatenate([s_sel.reshape(b, h, MOBA_QCHUNK, n_k), s_own], axis=-1)
        p = jax.nn.softmax(s_all, axis=-1)
        p_sel = p[..., :n_k].reshape(b, h, MOBA_QCHUNK, n_sel, MOBA_BLOCK)
        p_own = p[..., n_k:]
        return (jnp.einsum('bhqnk,bhqnkd->bhqd', p_sel, v_sel)
                + jnp.einsum('bhqk,bhkd->bhqd', p_own, v_own))

    outs = lax.map(one_chunk, jnp.arange(s // MOBA_QCHUNK))
    return outs.transpose(1, 2, 0, 3, 4).reshape(b, h, s, dh)


def retention(q, k, v):
    b, h, s, dh = q.shape
    L = RET_CHUNK
    nc = s // L
    log_gamma = jnp.log(1.0 - jnp.exp2(-5.0 - jnp.arange(h, dtype=jnp.float32)))
    k = k * (dh ** -0.5)
    qc = q.reshape(b, h, nc, L, dh)
    kc = k.reshape(b, h, nc, L, dh)
    vc = v.reshape(b, h, nc, L, dh)
    pos = jnp.arange(L, dtype=jnp.float32)
    diff = pos[:, None] - pos[None, :]
    decay = jnp.where(diff >= 0, jnp.exp(log_gamma[:, None, None] * jnp.maximum(diff, 0.0)), 0.0)
    scores = jnp.einsum('bhcnd,bhcmd->bhcnm', qc, kc) * decay[None, :, None]
    o_intra = jnp.einsum('bhcnm,bhcme->bhcne', scores, vc)
    zeta = jnp.exp(log_gamma[:, None] * (L - 1.0 - pos))
    kv = jnp.einsum('bhcmd,bhcme->bhcde', kc * zeta[None, :, None, :, None], vc)
    chunk_decay = jnp.exp(log_gamma * L)[None, :, None, None]

    def step(state, kv_c):
        return chunk_decay * state + kv_c, state

    r0 = jnp.zeros((b, h, dh, dh), jnp.float32)
    _, r_prev = lax.scan(step, r0, kv.transpose(2, 0, 1, 3, 4))
    r_prev = r_prev.transpose(1, 2, 0, 3, 4)
    xi = jnp.exp(log_gamma[:, None] * (pos + 1.0))
    o_cross = jnp.einsum('bhcnd,bhcde->bhcne', qc, r_prev) * xi[None, :, None, :, None]
    return (o_intra + o_cross).reshape(b, h, s, dh)


def mlstm(q, k, v, i_pre, f_pre):
    b, h, s, dh = q.shape
    L = MLSTM_CHUNK
    nc = s // L
    k = k * (dh ** -0.5)
    to_chunks = lambda t: t.reshape(b, h, nc, L, dh).transpose(2, 0, 1, 3, 4)
    qc, kc, vc = to_chunks(q), to_chunks(k), to_chunks(v)
    ic = i_pre.reshape(b, h, nc, L).transpose(2, 0, 1, 3)
    bc = jnp.cumsum(jax.nn.log_sigmoid(f_pre).reshape(b, h, nc, L), axis=-1).transpose(2, 0, 1, 3)
    pos = jnp.arange(L)
    causal = pos[:, None] >= pos[None, :]

    def step(carry, inp):
        c_st, n_st, m_st = carry
        qt, kt, vt, it, bt = inp
        d_log = jnp.where(causal, bt[..., :, None] - bt[..., None, :] + it[..., None, :], -jnp.inf)
        inter_log = bt + m_st[..., None]
        m_t = jnp.maximum(inter_log, jnp.max(d_log, axis=-1))
        w_intra = jnp.exp(d_log - m_t[..., None])
        w_inter = jnp.exp(inter_log - m_t)
        sc = jnp.einsum('bhtd,bhsd->bhts', qt, kt) * w_intra
        num = (jnp.einsum('bhts,bhse->bhte', sc, vt)
               + w_inter[..., None] * jnp.einsum('bhtd,bhde->bhte', qt, c_st))
        den = jnp.sum(sc, axis=-1) + w_inter * jnp.einsum('bhtd,bhd->bht', qt, n_st)
        h_t = num / jnp.maximum(jnp.abs(den), jnp.exp(-m_t))[..., None]
        b_last = bt[..., -1]
        g = b_last[..., None] - bt + it
        m_new = jnp.maximum(b_last + m_st, jnp.max(g, axis=-1))
        a = jnp.exp(b_last + m_st - m_new)
        w = jnp.exp(g - m_new[..., None])
        c_new = a[..., None, None] * c_st + jnp.einsum('bhs,bhsd,bhse->bhde', w, kt, vt)
        n_new = a[..., None] * n_st + jnp.einsum('bhs,bhsd->bhd', w, kt)
        return (c_new, n_new, m_new), h_t

    init = (jnp.zeros((b, h, dh, dh), jnp.float32), jnp.zeros((b, h, dh), jnp.float32),
            jnp.zeros((b, h), jnp.float32))
    _, hs = lax.scan(step, init, (qc, kc, vc, ic, bc))
    return hs.transpose(1, 2, 0, 3, 4).reshape(b, h, s, dh)


def causal_dwconv(x, w, bias):
    c = x.shape[-1]
    y = lax.conv_general_dilated(x, w.astype(x.dtype)[:, None, :], window_strides=(1,),
                                 padding=[(CONV_K - 1, 0)], dimension_numbers=('NWC', 'WIO', 'NWC'),
                                 feature_group_count=c)
    return y + bias.astype(x.dtype)


def hybrid_layer(x, norm_w, w_in, conv_w, conv_b, gate_bias, ret_norm_w, mlstm_norm_w, w_out):
    b, s, _ = x.shape
    s_pad = -(-s // PAD_MULT) * PAD_MULT
    h = rmsnorm(x, norm_w)
    proj = jnp.matmul(h, w_in).astype(jnp.float32)
    proj = jnp.pad(proj, ((0, 0), (0, s_pad - s), (0, 0)))
    cuts = np.cumsum(SPLIT_SIZES)[:-1].tolist()
    (aq, ak, av, ag, bq, bk, bv, bg, cq, ck, cv, co, cg, ci, cf) = jnp.split(proj, cuts, axis=-1)

    ya = moba_attention(split_heads(aq, A_HEADS), split_heads(ak, A_HEADS), split_heads(av, A_HEADS))
    ya = merge_heads(ya) * jax.nn.silu(ag)

    pos = jnp.arange(s_pad, dtype=jnp.float32)
    yb = retention(rotate(split_heads(bq, B_HEADS), pos), rotate(split_heads(bk, B_HEADS), pos),
                   split_heads(bv, B_HEADS))
    yb = head_norm(yb, ret_norm_w) * jax.nn.silu(bg)

    qk = jax.nn.silu(causal_dwconv(jnp.concatenate([cq, ck], axis=-1), conv_w, conv_b))
    cq_c, ck_c = qk[..., :C_WIDTH], qk[..., C_WIDTH:]
    gb = gate_bias.astype(jnp.float32)
    i_pre = ci.transpose(0, 2, 1) + gb[:C_HEADS][None, :, None]
    f_pre = cf.transpose(0, 2, 1) + gb[C_HEADS:][None, :, None]
    yc = mlstm(split_heads(cq_c, C_HEADS), split_heads(ck_c, C_HEADS), split_heads(cv, C_HEADS),
               i_pre, f_pre)
    yc = yc * split_heads(jax.nn.sigmoid(co), C_HEADS)
    yc = head_norm(yc, mlstm_norm_w) * jax.nn.silu(cg)

    y = jnp.concatenate([ya, yb, yc], axis=-1)[:, :s].astype(x.dtype)
    return x + jnp.matmul(y, w_out)


def setup_inputs(seed: int = 0) -> dict:
    key = jax.random.key(seed)
    ks = jax.random.split(key, 12)
    f32 = jnp.float32
    x = jax.random.normal(ks[0], (BATCH, SEQ, D_MODEL), f32)
    norm_w = 1.0 + 0.02 * jax.random.normal(ks[1], (DEPTH, D_MODEL), f32)
    w_in = jax.random.normal(ks[2], (DEPTH, D_MODEL, N_IN), f32) * D_MODEL ** -0.5
    conv_w = jax.random.normal(ks[3], (DEPTH, CONV_K, 2 * C_WIDTH), f32) * CONV_K ** -0.5
    conv_b = 0.02 * jax.random.normal(ks[4], (DEPTH, 2 * C_WIDTH), f32)
    i_bias = 0.1 * jax.random.normal(ks[5], (DEPTH, C_HEADS), f32)
    f_bias = jnp.linspace(3.0, 6.0, C_HEADS, dtype=f32)[None, :] + 0.1 * jax.random.normal(ks[6], (DEPTH, C_HEADS), f32)
    gate_bias = jnp.concatenate([i_bias, f_bias], axis=-1)
    ret_norm_w = 1.0 + 0.02 * jax.random.normal(ks[7], (DEPTH, B_WIDTH), f32)
    mlstm_norm_w = 1.0 + 0.02 * jax.random.normal(ks[8], (DEPTH, C_WIDTH), f32)
    w_out = jax.random.normal(ks[9], (DEPTH, MIX_WIDTH, D_MODEL), f32) * MIX_WIDTH ** -0.5
    final_norm_w = 1.0 + 0.02 * jax.random.normal(ks[10], (D_MODEL,), f32)
    return {"x": x, "norm_w": norm_w, "w_in": w_in, "conv_w": conv_w, "conv_b": conv_b,
            "gate_bias": gate_bias, "ret_norm_w": ret_norm_w, "mlstm_norm_w": mlstm_norm_w,
            "w_out": w_out, "final_norm_w": final_norm_w}


def reference(x, norm_w, w_in, conv_w, conv_b, gate_bias, ret_norm_w, mlstm_norm_w, w_out, final_norm_w):
    for layer in range(DEPTH):
        x = hybrid_layer(x, norm_w[layer], w_in[layer], conv_w[layer], conv_b[layer], gate_bias[layer],
                         ret_norm_w[layer], mlstm_norm_w[layer], w_out[layer])
    return rmsnorm(x, final_norm_w)
```

```python
import functools

import numpy as np
import jax
import jax.numpy as jnp
from jax import lax
from jax.experimental import pallas as pl
from jax.experimental.pallas import tpu as pltpu

D_MODEL = 1024
HEAD_DIM = 64
A_HEADS, B_HEADS, C_HEADS = 8, 4, 4
A_WIDTH, B_WIDTH, C_WIDTH = A_HEADS * HEAD_DIM, B_HEADS * HEAD_DIM, C_HEADS * HEAD_DIM
MOBA_BLOCK = 256
MOBA_TOPK = 3
RET_CHUNK = 256
MLSTM_CHUNK = 128
CONV_K = 4
RMS_EPS = 1e-6
HEAD_NORM_EPS = 1e-5
ROPE_BASE = 10000.0
N_MAIN = 4 * A_WIDTH + 4 * B_WIDTH + 5 * C_WIDTH
QK_SCALE = HEAD_DIM ** -0.5

LANES = 128
CONV_PAD = 8
PROJ_ROWS = 512
OUT_ROWS = 512
VMEM_LIMIT = 56 * 1024 * 1024

BF16 = jnp.bfloat16
F32 = jnp.float32
NEG_INF = float("-inf")


def _silu(t):
    return t * jax.nn.sigmoid(t)


def _swap_half_heads(t):
    lane = lax.broadcasted_iota(jnp.int32, t.shape, 1)
    first_half = (lane % HEAD_DIM) < (HEAD_DIM // 2)
    return jnp.where(first_half, pltpu.roll(t, LANES - HEAD_DIM // 2, 1), pltpu.roll(t, HEAD_DIM // 2, 1))


def _proj_kernel(x_ref, nw_ref, w_ref, wg_ref, cos_ref, sin_ref, cw_ref, cb_ref,
                 qT_ref, ka_ref, vT_ref, ga_ref, kmean_ref, ret_ref, mls_ref, gates_ref,
                 h_scr, ext_scr):
    i = pl.program_id(0)
    tm = x_ref.shape[0]
    nblk = tm // MOBA_BLOCK

    x = x_ref[...]
    ms = jnp.mean(x * x, axis=-1, keepdims=True)
    h_scr[...] = ((x * lax.rsqrt(ms + RMS_EPS)) * nw_ref[...]).astype(BF16)

    def seg(lo, width):
        return jnp.dot(h_scr[...], w_ref[:, lo:lo + width], preferred_element_type=F32)

    aq = seg(0, A_WIDTH) * QK_SCALE
    ak = seg(A_WIDTH, A_WIDTH)
    av = seg(2 * A_WIDTH, A_WIDTH)
    ka_ref[...] = ak.astype(BF16)
    ga_ref[...] = seg(3 * A_WIDTH, A_WIDTH).astype(BF16)
    for r in range(nblk):
        rows = slice(r * MOBA_BLOCK, (r + 1) * MOBA_BLOCK)
        qT_ref[r] = aq[rows].T.astype(BF16)
        vT_ref[r] = av[rows].T.astype(BF16)
        kmean_ref[r] = jnp.mean(ak[rows], axis=0, keepdims=True)

    b0 = 4 * A_WIDTH
    cos = cos_ref[...]
    sin = sin_ref[...]
    for t in range(2 * B_WIDTH // LANES):
        v = seg(b0 + t * LANES, LANES)
        rot = v * cos + _swap_half_heads(v) * sin
        if t >= B_WIDTH // LANES:
            rot = rot * QK_SCALE
        ret_ref[:, t * LANES:(t + 1) * LANES] = rot.astype(BF16)
    ret_ref[:, 2 * B_WIDTH:4 * B_WIDTH] = seg(b0 + 2 * B_WIDTH, 2 * B_WIDTH).astype(BF16)

    c0 = b0 + 4 * B_WIDTH

    @pl.when(i == 0)
    def _():
        ext_scr[0:CONV_PAD, :] = jnp.zeros((CONV_PAD, 2 * C_WIDTH), F32)

    ext_scr[CONV_PAD:CONV_PAD + tm, :] = seg(c0, 2 * C_WIDTH)
    y = jnp.zeros((tm, 2 * C_WIDTH), F32) + cb_ref[...]
    for tap in range(CONV_K):
        lo = CONV_PAD - (CONV_K - 1) + tap
        y = y + cw_ref[tap:tap + 1, :] * ext_scr[lo:lo + tm, :]
    ext_scr[0:CONV_PAD, :] = ext_scr[tm:tm + CONV_PAD, :]
    qk = _silu(y)
    mls_ref[:, 0:C_WIDTH] = qk[:, 0:C_WIDTH].astype(BF16)
    mls_ref[:, C_WIDTH:2 * C_WIDTH] = (qk[:, C_WIDTH:] * QK_SCALE).astype(BF16)
    mls_ref[:, 2 * C_WIDTH:5 * C_WIDTH] = seg(c0 + 2 * C_WIDTH, 3 * C_WIDTH).astype(BF16)

    gates_ref[...] = jnp.dot(h_scr[...], wg_ref[...], preferred_element_type=F32)


def _proj(x, norm_w, w_main, w_gate, cos_t, sin_t, conv_w, conv_b):
    s = x.shape[0]
    tm = PROJ_ROWS
    nblk = tm // MOBA_BLOCK
    nb = s // MOBA_BLOCK
    full = lambda shape: pl.BlockSpec(shape, lambda i: (0,) * len(shape))
    return pl.pallas_call(
        _proj_kernel,
        grid=(s // tm,),
        in_specs=[
            pl.BlockSpec((tm, D_MODEL), lambda i: (i, 0)),
            full((1, D_MODEL)),
            full((D_MODEL, N_MAIN)),
            full((D_MODEL, LANES)),
            pl.BlockSpec((tm, LANES), lambda i: (i, 0)),
            pl.BlockSpec((tm, LANES), lambda i: (i, 0)),
            full((CONV_K, 2 * C_WIDTH)),
            full((1, 2 * C_WIDTH)),
        ],
        out_specs=[
            pl.BlockSpec((nblk, A_WIDTH, MOBA_BLOCK), lambda i: (i, 0, 0)),
            pl.BlockSpec((tm, A_WIDTH), lambda i: (i, 0)),
            pl.BlockSpec((nblk, A_WIDTH, MOBA_BLOCK), lambda i: (i, 0, 0)),
            pl.BlockSpec((tm, A_WIDTH), lambda i: (i, 0)),
            pl.BlockSpec((nblk, 1, A_WIDTH), lambda i: (i, 0, 0)),
            pl.BlockSpec((tm, 4 * B_WIDTH), lambda i: (i, 0)),
            pl.BlockSpec((tm, 5 * C_WIDTH), lambda i: (i, 0)),
            pl.BlockSpec((tm, LANES), lambda i: (i, 0)),
        ],
        out_shape=[
            jax.ShapeDtypeStruct((nb, A_WIDTH, MOBA_BLOCK), BF16),
            jax.ShapeDtypeStruct((s, A_WIDTH), BF16),
            jax.ShapeDtypeStruct((nb, A_WIDTH, MOBA_BLOCK), BF16),
            jax.ShapeDtypeStruct((s, A_WIDTH), BF16),
            jax.ShapeDtypeStruct((nb, 1, A_WIDTH), F32),
            jax.ShapeDtypeStruct((s, 4 * B_WIDTH), BF16),
            jax.ShapeDtypeStruct((s, 5 * C_WIDTH), BF16),
            jax.ShapeDtypeStruct((s, LANES), F32),
        ],
        scratch_shapes=[
            pltpu.VMEM((tm, D_MODEL), BF16),
            pltpu.VMEM((tm + CONV_PAD, 2 * C_WIDTH), F32),
        ],
        compiler_params=pltpu.CompilerParams(
            dimension_semantics=("arbitrary",), vmem_limit_bytes=VMEM_LIMIT),
        name="proj",
    )(x, norm_w, w_main, w_gate, cos_t, sin_t, conv_w, conv_b)


def _moba_kernel(slopes_ref, qT_ref, k_ref, vT_ref, kmean_ref, g_ref, o_ref,
                 qm_scr, bias_scr, sel_scr, m_scr, l_scr, acc_scr):
    p = pl.program_id(0)
    j = pl.program_id(1)
    blk = MOBA_BLOCK
    nb = kmean_ref.shape[0]

    qT2 = qT_ref[0]
    dim_row = lax.broadcasted_iota(jnp.int32, (LANES, blk), 0)
    blk_row = lax.broadcasted_iota(jnp.int32, (nb, blk), 0)
    key_off = lax.broadcasted_iota(jnp.int32, (blk, blk), 0)
    qry_off = lax.broadcasted_iota(jnp.int32, (blk, blk), 1)
    kmean = kmean_ref[...].astype(BF16)
    k_own = k_ref[pl.ds(pl.multiple_of(j * blk, blk), blk), :]
    vT_own = vT_ref[j]

    for hh in range(2):
        slope = slopes_ref[2 * p + hh]
        qm = jnp.where((dim_row // HEAD_DIM) == hh, qT2, jnp.zeros_like(qT2))
        qm_scr[hh] = qm
        bias = slope * key_off.astype(F32)
        bias_scr[hh] = bias

        gate = jnp.dot(kmean, qm, preferred_element_type=F32)
        work = jnp.where(blk_row < j, gate, NEG_INF)
        sel = jnp.zeros((nb, blk), F32)
        for r in range(MOBA_TOPK):
            mx = jnp.max(work, axis=0, keepdims=True)
            idx = jnp.min(jnp.where(work == mx, blk_row, nb), axis=0, keepdims=True)
            pick = blk_row == idx
            sel = jnp.where(jnp.logical_and(pick, r < j), 1.0, sel)
            work = jnp.where(pick, NEG_INF, work)
        sel_scr[hh] = sel

        s = jnp.dot(k_own, qm, preferred_element_type=F32) + bias
        s = jnp.where(key_off <= qry_off, s, NEG_INF)
        m = jnp.max(s, axis=0, keepdims=True)
        pm = jnp.exp(s - m)
        m_scr[hh] = m
        l_scr[hh] = jnp.sum(pm, axis=0, keepdims=True)
        acc_scr[hh] = jnp.dot(vT_own[hh * HEAD_DIM:(hh + 1) * HEAD_DIM, :], pm.astype(BF16),
                              preferred_element_type=F32)

    def body(n, carry):
        k_blk = k_ref[pl.ds(pl.multiple_of(n * blk, blk), blk), :]
        vT_blk = vT_ref[n]
        dist = ((n - j) * blk).astype(F32)
        for hh in range(2):
            c = slopes_ref[2 * p + hh] * dist
            s = jnp.dot(k_blk, qm_scr[hh], preferred_element_type=F32) + bias_scr[hh]
            chosen = sel_scr[hh, pl.ds(n, 1), :] > 0.0
            m_old = m_scr[hh]
            m_new = jnp.where(chosen, jnp.maximum(m_old, jnp.max(s, axis=0, keepdims=True) + c), m_old)
            alpha = jnp.exp(m_old - m_new)
            shift = jnp.where(chosen, m_new - c, jnp.inf)
            pm = jnp.exp(s - shift)
            m_scr[hh] = m_new
            l_scr[hh] = alpha * l_scr[hh] + jnp.sum(pm, axis=0, keepdims=True)
            acc_scr[hh] = alpha * acc_scr[hh] + jnp.dot(
                vT_blk[hh * HEAD_DIM:(hh + 1) * HEAD_DIM, :], pm.astype(BF16),
                preferred_element_type=F32)
        return carry

    lax.fori_loop(0, j, body, 0)

    oT = jnp.concatenate([acc_scr[0] / l_scr[0], acc_scr[1] / l_scr[1]], axis=0)
    o_ref[...] = (oT.T * _silu(g_ref[...].astype(F32))).astype(o_ref.dtype)


def _moba(slopes, qT, ka, vT, kmean, ga):
    s = ka.shape[0]
    nb = s // MOBA_BLOCK
    blk = MOBA_BLOCK
    npair = A_HEADS // 2
    return pl.pallas_call(
        _moba_kernel,
        grid=(npair, nb),
        in_specs=[
            pl.BlockSpec(memory_space=pltpu.SMEM),
            pl.BlockSpec((1, LANES, blk), lambda p, j: (j, p, 0)),
            pl.BlockSpec((s, LANES), lambda p, j: (0, p)),
            pl.BlockSpec((nb, LANES, blk), lambda p, j: (0, p, 0)),
            pl.BlockSpec((nb, LANES), lambda p, j: (0, p)),
            pl.BlockSpec((blk, LANES), lambda p, j: (j, p)),
        ],
        out_specs=pl.BlockSpec((blk, LANES), lambda p, j: (j, p)),
        out_shape=jax.ShapeDtypeStruct((s, A_WIDTH), BF16),
        scratch_shapes=[
            pltpu.VMEM((2, LANES, blk), BF16),
            pltpu.VMEM((2, blk, blk), F32),
            pltpu.VMEM((2, nb, blk), F32),
            pltpu.VMEM((2, 1, blk), F32),
            pltpu.VMEM((2, 1, blk), F32),
            pltpu.VMEM((2, HEAD_DIM, blk), F32),
        ],
        compiler_params=pltpu.CompilerParams(
            dimension_semantics=("arbitrary", "arbitrary"), vmem_limit_bytes=VMEM_LIMIT),
        name="moba",
    )(slopes, qT, ka, vT, kmean, ga)


def _head_norm_pair(y, lane_head):
    mu = jnp.zeros_like(y)
    for hh in range(2):
        in_h = lane_head == hh
        mu_h = jnp.sum(jnp.where(in_h, y, 0.0), axis=-1, keepdims=True) * (1.0 / HEAD_DIM)
        mu = jnp.where(in_h, mu_h, mu)
    d = y - mu
    var = jnp.zeros_like(y)
    for hh in range(2):
        in_h = lane_head == hh
        var_h = jnp.sum(jnp.where(in_h, d * d, 0.0), axis=-1, keepdims=True) * (1.0 / HEAD_DIM)
        var = jnp.where(in_h, var_h, var)
    return d * lax.rsqrt(var + HEAD_NORM_EPS)


def _ret_kernel(x_ref, decay_ref, zeta_ref, xi_ref, cd_ref, nw_ref, o_ref, state_scr):
    c = pl.program_id(0)
    L = RET_CHUNK

    @pl.when(c == 0)
    def _():
        state_scr[...] = jnp.zeros_like(state_scr)

    lane_head = lax.broadcasted_iota(jnp.int32, (L, LANES), 1) // HEAD_DIM
    sq_row = lax.broadcasted_iota(jnp.int32, (LANES, LANES), 0) // HEAD_DIM
    sq_col = lax.broadcasted_iota(jnp.int32, (LANES, LANES), 1) // HEAD_DIM
    for t in range(B_WIDTH // LANES):
        cols = slice(t * LANES, (t + 1) * LANES)
        q2 = x_ref[:, t * LANES:(t + 1) * LANES]
        k2 = x_ref[:, B_WIDTH + t * LANES:B_WIDTH + (t + 1) * LANES]
        v2 = x_ref[:, 2 * B_WIDTH + t * LANES:2 * B_WIDTH + (t + 1) * LANES]
        g2 = x_ref[:, 3 * B_WIDTH + t * LANES:3 * B_WIDTH + (t + 1) * LANES]
        state = state_scr[t]
        y = jnp.dot(q2, state.astype(BF16), preferred_element_type=F32) * xi_ref[:, cols]
        for hh in range(2):
            qm = jnp.where(lane_head == hh, q2, jnp.zeros_like(q2))
            sc = lax.dot_general(qm, k2, (((1,), (1,)), ((), ())), preferred_element_type=F32)
            sc = sc * decay_ref[2 * t + hh]
            o_h = jnp.dot(sc.astype(BF16), v2, preferred_element_type=F32)
            y = y + jnp.where(lane_head == hh, o_h, 0.0)
        kz = (k2.astype(F32) * zeta_ref[:, cols]).astype(BF16)
        kv = lax.dot_general(kz, v2, (((0,), (0,)), ((), ())), preferred_element_type=F32)
        state_scr[t] = cd_ref[t] * state + jnp.where(sq_row == sq_col, kv, 0.0)
        yn = _head_norm_pair(y, lane_head) * nw_ref[:, cols]
        o_ref[:, cols] = (yn * _silu(g2.astype(F32))).astype(o_ref.dtype)


def _ret(ret_in, decay, zeta, xi, cd, norm_w):
    s = ret_in.shape[0]
    L = RET_CHUNK
    full = lambda shape: pl.BlockSpec(shape, lambda c: (0,) * len(shape))
    return pl.pallas_call(
        _ret_kernel,
        grid=(s // L,),
        in_specs=[
            pl.BlockSpec((L, 4 * B_WIDTH), lambda c: (c, 0)),
            full((B_HEADS, L, L)),
            full((L, B_WIDTH)),
            full((L, B_WIDTH)),
            full((B_WIDTH // LANES, LANES, LANES)),
            full((1, B_WIDTH)),
        ],
        out_specs=pl.BlockSpec((L, B_WIDTH), lambda c: (c, 0)),
        out_shape=jax.ShapeDtypeStruct((s, B_WIDTH), BF16),
        scratch_shapes=[pltpu.VMEM((B_WIDTH // LANES, LANES, LANES), F32)],
        compiler_params=pltpu.CompilerParams(
            dimension_semantics=("arbitrary",), vmem_limit_bytes=VMEM_LIMIT),
        name="ret",
    )(ret_in, decay, zeta, xi, cd, norm_w)


def _mlstm_kernel(x_ref, gates_ref, gb_ref, nw_ref, o_ref, c_scr, n_scr, m_scr):
    ci = pl.program_id(0)
    L = MLSTM_CHUNK

    @pl.when(ci == 0)
    def _():
        c_scr[...] = jnp.zeros_like(c_scr)
        n_scr[...] = jnp.zeros_like(n_scr)
        m_scr[...] = jnp.zeros_like(m_scr)

    lane_head = lax.broadcasted_iota(jnp.int32, (L, LANES), 1) // HEAD_DIM
    sq_row = lax.broadcasted_iota(jnp.int32, (LANES, LANES), 0) // HEAD_DIM
    sq_col = lax.broadcasted_iota(jnp.int32, (LANES, LANES), 1) // HEAD_DIM
    t_row = lax.broadcasted_iota(jnp.int32, (L, L), 0)
    t_col = lax.broadcasted_iota(jnp.int32, (L, L), 1)
    causal = t_row >= t_col
    lower = causal.astype(F32)

    pre = gates_ref[...] + gb_ref[...]
    logsig = jnp.minimum(pre, 0.0) - jnp.log1p(jnp.exp(-jnp.abs(pre)))
    bsum = jnp.dot(lower, logsig, preferred_element_type=F32, precision=lax.Precision.HIGHEST)
    pre_t = pre.T
    bsum_t = bsum.T

    for t in range(C_WIDTH // LANES):
        cols = slice(t * LANES, (t + 1) * LANES)
        q2 = x_ref[:, t * LANES:(t + 1) * LANES]
        k2 = x_ref[:, C_WIDTH + t * LANES:C_WIDTH + (t + 1) * LANES]
        v2 = x_ref[:, 2 * C_WIDTH + t * LANES:2 * C_WIDTH + (t + 1) * LANES]
        o2 = x_ref[:, 3 * C_WIDTH + t * LANES:3 * C_WIDTH + (t + 1) * LANES]
        g2 = x_ref[:, 4 * C_WIDTH + t * LANES:4 * C_WIDTH + (t + 1) * LANES]
        c_st = c_scr[t]
        n_st = n_scr[t]
        qc = jnp.dot(q2, c_st.astype(BF16), preferred_element_type=F32)
        qn = q2.astype(F32) * n_st
        k2f = k2.astype(F32)

        num = jnp.zeros((L, LANES), F32)
        den = jnp.zeros((L, LANES), F32)
        emm = jnp.zeros((L, LANES), F32)
        a_lane = jnp.zeros((1, LANES), F32)
        w_lane = jnp.zeros((L, LANES), F32)
        m_lane = jnp.zeros((1, LANES), F32)
        for hh in range(2):
            h = 2 * t + hh
            in_h = lane_head == hh
            i_col = pre[:, h:h + 1]
            b_col = bsum[:, C_HEADS + h:C_HEADS + h + 1]
            i_row = pre_t[h:h + 1, :]
            b_row = bsum_t[C_HEADS + h:C_HEADS + h + 1, :]
            m_st = m_scr[t, :, hh * HEAD_DIM:hh * HEAD_DIM + 1]

            d_log = jnp.where(causal, b_col - b_row + i_row, NEG_INF)
            inter_log = b_col + m_st
            m_t = jnp.maximum(inter_log, jnp.max(d_log, axis=-1, keepdims=True))
            w_intra = jnp.exp(d_log - m_t)
            w_inter = jnp.exp(inter_log - m_t)
            qm = jnp.where(in_h, q2, jnp.zeros_like(q2))
            sc = lax.dot_general(qm, k2, (((1,), (1,)), ((), ())), preferred_element_type=F32) * w_intra
            num_h = jnp.dot(sc.astype(BF16), v2, preferred_element_type=F32) + w_inter * qc
            den_h = (jnp.sum(sc, axis=-1, keepdims=True)
                     + w_inter * jnp.sum(jnp.where(in_h, qn, 0.0), axis=-1, keepdims=True))
            num = jnp.where(in_h, num_h, num)
            den = jnp.where(in_h, den_h, den)
            emm = jnp.where(in_h, jnp.exp(-m_t), emm)

            b_last = b_col[L - 1:L, :]
            g_col = b_last - b_col + i_col
            m_new = jnp.maximum(b_last + m_st, jnp.max(g_col, axis=0, keepdims=True))
            a = jnp.exp(b_last + m_st - m_new)
            w_col = jnp.exp(g_col - m_new)
            in_h1 = in_h[0:1, :]
            a_lane = jnp.where(in_h1, a, a_lane)
            m_lane = jnp.where(in_h1, m_new, m_lane)
            w_lane = jnp.where(in_h, w_col, w_lane)

        h_t = num / jnp.maximum(jnp.abs(den), emm)
        kw = k2f * w_lane
        kv = lax.dot_general(kw.astype(BF16), v2, (((0,), (0,)), ((), ())), preferred_element_type=F32)
        a_col = jnp.transpose(jnp.broadcast_to(a_lane, (8, LANES)))[:, 0:1]
        c_scr[t] = a_col * c_st + jnp.where(sq_row == sq_col, kv, 0.0)
        n_scr[t] = a_lane * n_st + jnp.sum(kw, axis=0, keepdims=True)
        m_scr[t] = m_lane

        y = h_t * jax.nn.sigmoid(o2.astype(F32))
        yn = _head_norm_pair(y, lane_head) * nw_ref[:, cols]
        o_ref[:, cols] = (yn * _silu(g2.astype(F32))).astype(o_ref.dtype)


def _mlstm(mls_in, gates, gate_bias, norm_w):
    s = mls_in.shape[0]
    L = MLSTM_CHUNK
    npair = C_WIDTH // LANES
    full = lambda shape: pl.BlockSpec(shape, lambda c: (0,) * len(shape))
    return pl.pallas_call(
        _mlstm_kernel,
        grid=(s // L,),
        in_specs=[
            pl.BlockSpec((L, 5 * C_WIDTH), lambda c: (c, 0)),
            pl.BlockSpec((L, LANES), lambda c: (c, 0)),
            full((1, LANES)),
            full((1, C_WIDTH)),
        ],
        out_specs=pl.BlockSpec((L, C_WIDTH), lambda c: (c, 0)),
        out_shape=jax.ShapeDtypeStruct((s, C_WIDTH), BF16),
        scratch_shapes=[
            pltpu.VMEM((npair, LANES, LANES), F32),
            pltpu.VMEM((npair, 1, LANES), F32),
            pltpu.VMEM((npair, 1, LANES), F32),
        ],
        compiler_params=pltpu.CompilerParams(
            dimension_semantics=("arbitrary",), vmem_limit_bytes=VMEM_LIMIT),
        name="mlstm",
    )(mls_in, gates, gate_bias, norm_w)


def _outproj_kernel(x_ref, ya_ref, yb_ref, yc_ref, wa_ref, wb_ref, wc_ref, fw_ref, o_ref, *, final_norm):
    acc = x_ref[...]
    acc = acc + jnp.dot(ya_ref[...], wa_ref[...], preferred_element_type=F32)
    acc = acc + jnp.dot(yb_ref[...], wb_ref[...], preferred_element_type=F32)
    acc = acc + jnp.dot(yc_ref[...], wc_ref[...], preferred_element_type=F32)
    if final_norm:
        ms = jnp.mean(acc * acc, axis=-1, keepdims=True)
        acc = (acc * lax.rsqrt(ms + RMS_EPS)) * fw_ref[...]
    o_ref[...] = acc


def _outproj(x, ya, yb, yc, wa, wb, wc, final_w, final_norm):
    s = x.shape[0]
    tm = OUT_ROWS
    full = lambda shape: pl.BlockSpec(shape, lambda i: (0,) * len(shape))
    rows = lambda width: pl.BlockSpec((tm, width), lambda i: (i, 0))
    return pl.pallas_call(
        functools.partial(_outproj_kernel, final_norm=final_norm),
        grid=(s // tm,),
        in_specs=[rows(D_MODEL), rows(A_WIDTH), rows(B_WIDTH), rows(C_WIDTH),
                  full((A_WIDTH, D_MODEL)), full((B_WIDTH, D_MODEL)), full((C_WIDTH, D_MODEL)),
                  full((1, D_MODEL))],
        out_specs=rows(D_MODEL),
        out_shape=jax.ShapeDtypeStruct((s, D_MODEL), F32),
        compiler_params=pltpu.CompilerParams(
            dimension_semantics=("arbitrary",), vmem_limit_bytes=VMEM_LIMIT),
        name="outproj",
    )(x, ya, yb, yc, wa, wb, wc, final_w)


def _rope_tables(s):
    half = HEAD_DIM // 2
    pos = jnp.arange(s, dtype=F32)
    theta = 1.0 / (ROPE_BASE ** jnp.linspace(0.0, 1.0, half, dtype=F32))
    ang = pos[:, None] * theta[None, :]
    cos, sin = jnp.cos(ang), jnp.sin(ang)
    cos_t = jnp.tile(cos, (1, LANES // half))
    sin_t = jnp.tile(jnp.concatenate([-sin, sin], axis=-1), (1, LANES // HEAD_DIM))
    return cos_t, sin_t


def _retention_tables():
    L = RET_CHUNK
    h = B_HEADS
    log_gamma = jnp.log(1.0 - jnp.exp2(-5.0 - jnp.arange(h, dtype=F32)))
    pos = jnp.arange(L, dtype=F32)
    diff = pos[:, None] - pos[None, :]
    decay = jnp.where(diff >= 0, jnp.exp(log_gamma[:, None, None] * jnp.maximum(diff, 0.0)), 0.0)
    zeta = jnp.exp(log_gamma[:, None] * (L - 1.0 - pos))
    xi = jnp.exp(log_gamma[:, None] * (pos + 1.0))
    chunk_decay = jnp.exp(log_gamma * L)
    per_lane = lambda t: jnp.repeat(t.T, HEAD_DIM, axis=1)
    cd_lane = jnp.repeat(chunk_decay, HEAD_DIM).reshape(B_WIDTH // LANES, LANES, 1)
    cd = jnp.broadcast_to(cd_lane, (B_WIDTH // LANES, LANES, LANES))
    return decay, per_lane(zeta), per_lane(xi), cd


def _alibi_slopes():
    return jnp.exp2(-8.0 * jnp.arange(1, A_HEADS + 1, dtype=F32) / A_HEADS)


def kernel(x, norm_w, w_in, conv_w, conv_b, gate_bias, ret_norm_w, mlstm_norm_w, w_out, final_norm_w):
    b, s, d = x.shape
    assert b == 1 and d == D_MODEL and s % PROJ_ROWS == 0 and s % OUT_ROWS == 0
    depth = w_in.shape[0]
    cos_t, sin_t = _rope_tables(s)
    decay, zeta, xi, cd = _retention_tables()
    slopes = _alibi_slopes()

    w_main = w_in[:, :, :N_MAIN].astype(BF16)
    w_gate = jnp.pad(w_in[:, :, N_MAIN:], ((0, 0), (0, 0), (0, LANES - 2 * C_HEADS)))
    w_gate = w_gate.astype(BF16)
    w_out_b = w_out.astype(BF16)
    gb = jnp.pad(gate_bias.astype(F32), ((0, 0), (0, LANES - 2 * C_HEADS)))

    xs = x[0]
    for layer in range(depth):
        qT, ka, vT, ga, kmean, ret_in, mls_in, gates = _proj(
            xs, norm_w[layer][None, :], w_main[layer], w_gate[layer], cos_t, sin_t,
            conv_w[layer], conv_b[layer][None, :])
        ya = _moba(slopes, qT, ka, vT, kmean.reshape(-1, A_WIDTH), ga)
        yb = _ret(ret_in, decay, zeta, xi, cd, ret_norm_w[layer][None, :])
        yc = _mlstm(mls_in, gates, gb[layer][None, :], mlstm_norm_w[layer][None, :])
        wo = w_out_b[layer]
        xs = _outproj(xs, ya, yb, yc, wo[:A_WIDTH], wo[A_WIDTH:A_WIDTH + B_WIDTH], wo[A_WIDTH + B_WIDTH:],
                      final_norm_w[None, :], final_norm=(layer == depth - 1))
    return xs[None]
```

```python
import functools

import numpy as np
import jax
import jax.numpy as jnp
from jax import lax
from jax.experimental import pallas as pl
from jax.experimental.pallas import tpu as pltpu

D_MODEL = 1024
HEAD_DIM = 64
A_HEADS, B_HEADS, C_HEADS = 8, 4, 4
A_WIDTH, B_WIDTH, C_WIDTH = A_HEADS * HEAD_DIM, B_HEADS * HEAD_DIM, C_HEADS * HEAD_DIM
MOBA_BLOCK = 256
MOBA_TOPK = 3
MOBA_GROUP = 4
RET_CHUNK = 256
MLSTM_CHUNK = 128
CONV_K = 4
RMS_EPS = 1e-6
HEAD_NORM_EPS = 1e-5
ROPE_BASE = 10000.0
N_MAIN = 4 * A_WIDTH + 4 * B_WIDTH + 5 * C_WIDTH
QK_SCALE = HEAD_DIM ** -0.5
LOG2E = 1.4426950408889634

LANES = 128
CONV_PAD = 8
PROJ_ROWS = 512
OUT_ROWS = 512
VMEM_LIMIT = 56 * 1024 * 1024

BF16 = jnp.bfloat16
F32 = jnp.float32
NEG_INF = float("-inf")


def _silu(t):
    return t * jax.nn.sigmoid(t)


def _swap_half_heads(t):
    lane = lax.broadcasted_iota(jnp.int32, t.shape, 1)
    first_half = (lane % HEAD_DIM) < (HEAD_DIM // 2)
    return jnp.where(first_half, pltpu.roll(t, LANES - HEAD_DIM // 2, 1), pltpu.roll(t, HEAD_DIM // 2, 1))


def _proj_kernel(x_ref, nw_ref, w_ref, wg_ref, cos_ref, sin_ref, cw_ref, cb_ref,
                 qT_ref, ka_ref, vT_ref, ga_ref, kmean_ref, ret_ref, mls_ref, gates_ref,
                 h_scr, ext_scr):
    i = pl.program_id(0)
    tm = x_ref.shape[0]
    nblk = tm // MOBA_BLOCK

    x = x_ref[...]
    ms = jnp.mean(x * x, axis=-1, keepdims=True)
    h_scr[...] = ((x * lax.rsqrt(ms + RMS_EPS)) * nw_ref[...]).astype(BF16)

    def seg(lo, width):
        return jnp.dot(h_scr[...], w_ref[:, lo:lo + width], preferred_element_type=F32)

    aq = seg(0, A_WIDTH) * (QK_SCALE * LOG2E)
    ak = seg(A_WIDTH, A_WIDTH)
    av = seg(2 * A_WIDTH, A_WIDTH)
    ka_ref[...] = ak.astype(BF16)
    ga_ref[...] = seg(3 * A_WIDTH, A_WIDTH).astype(BF16)
    for r in range(nblk):
        rows = slice(r * MOBA_BLOCK, (r + 1) * MOBA_BLOCK)
        qT_ref[r] = aq[rows].T.astype(BF16)
        vT_ref[r] = av[rows].T.astype(BF16)
        kmean_ref[r] = jnp.mean(ak[rows], axis=0, keepdims=True)

    b0 = 4 * A_WIDTH
    cos = cos_ref[...]
    sin = sin_ref[...]
    for t in range(2 * B_WIDTH // LANES):
        v = seg(b0 + t * LANES, LANES)
        rot = v * cos + _swap_half_heads(v) * sin
        if t >= B_WIDTH // LANES:
            rot = rot * QK_SCALE
        ret_ref[:, t * LANES:(t + 1) * LANES] = rot.astype(BF16)
    ret_ref[:, 2 * B_WIDTH:4 * B_WIDTH] = seg(b0 + 2 * B_WIDTH, 2 * B_WIDTH).astype(BF16)

    c0 = b0 + 4 * B_WIDTH

    @pl.when(i == 0)
    def _():
        ext_scr[0:CONV_PAD, :] = jnp.zeros((CONV_PAD, 2 * C_WIDTH), F32)

    ext_scr[CONV_PAD:CONV_PAD + tm, :] = seg(c0, 2 * C_WIDTH)
    y = jnp.zeros((tm, 2 * C_WIDTH), F32) + cb_ref[...]
    for tap in range(CONV_K):
        lo = CONV_PAD - (CONV_K - 1) + tap
        y = y + cw_ref[tap:tap + 1, :] * ext_scr[lo:lo + tm, :]
    ext_scr[0:CONV_PAD, :] = ext_scr[tm:tm + CONV_PAD, :]
    qk = _silu(y)
    mls_ref[:, 0:C_WIDTH] = qk[:, 0:C_WIDTH].astype(BF16)
    mls_ref[:, C_WIDTH:2 * C_WIDTH] = (qk[:, C_WIDTH:] * QK_SCALE).astype(BF16)
    mls_ref[:, 2 * C_WIDTH:5 * C_WIDTH] = seg(c0 + 2 * C_WIDTH, 3 * C_WIDTH).astype(BF16)

    gates_ref[...] = jnp.dot(h_scr[...], wg_ref[...], preferred_element_type=F32)


def _proj(x, norm_w, w_main, w_gate, cos_t, sin_t, conv_w, conv_b):
    s = x.shape[0]
    tm = PROJ_ROWS
    nblk = tm // MOBA_BLOCK
    nb = s // MOBA_BLOCK
    full = lambda shape: pl.BlockSpec(shape, lambda i: (0,) * len(shape))
    return pl.pallas_call(
        _proj_kernel,
        grid=(s // tm,),
        in_specs=[
            pl.BlockSpec((tm, D_MODEL), lambda i: (i, 0)),
            full((1, D_MODEL)),
            full((D_MODEL, N_MAIN)),
            full((D_MODEL, LANES)),
            pl.BlockSpec((tm, LANES), lambda i: (i, 0)),
            pl.BlockSpec((tm, LANES), lambda i: (i, 0)),
            full((CONV_K, 2 * C_WIDTH)),
            full((1, 2 * C_WIDTH)),
        ],
        out_specs=[
            pl.BlockSpec((nblk, A_WIDTH, MOBA_BLOCK), lambda i: (i, 0, 0)),
            pl.BlockSpec((tm, A_WIDTH), lambda i: (i, 0)),
            pl.BlockSpec((nblk, A_WIDTH, MOBA_BLOCK), lambda i: (i, 0, 0)),
            pl.BlockSpec((tm, A_WIDTH), lambda i: (i, 0)),
            pl.BlockSpec((nblk, 1, A_WIDTH), lambda i: (i, 0, 0)),
            pl.BlockSpec((tm, 4 * B_WIDTH), lambda i: (i, 0)),
            pl.BlockSpec((tm, 5 * C_WIDTH), lambda i: (i, 0)),
            pl.BlockSpec((tm, LANES), lambda i: (i, 0)),
        ],
        out_shape=[
            jax.ShapeDtypeStruct((nb, A_WIDTH, MOBA_BLOCK), BF16),
            jax.ShapeDtypeStruct((s, A_WIDTH), BF16),
            jax.ShapeDtypeStruct((nb, A_WIDTH, MOBA_BLOCK), BF16),
            jax.ShapeDtypeStruct((s, A_WIDTH), BF16),
            jax.ShapeDtypeStruct((nb, 1, A_WIDTH), F32),
            jax.ShapeDtypeStruct((s, 4 * B_WIDTH), BF16),
            jax.ShapeDtypeStruct((s, 5 * C_WIDTH), BF16),
            jax.ShapeDtypeStruct((s, LANES), F32),
        ],
        scratch_shapes=[
            pltpu.VMEM((tm, D_MODEL), BF16),
            pltpu.VMEM((tm + CONV_PAD, 2 * C_WIDTH), F32),
        ],
        compiler_params=pltpu.CompilerParams(
            dimension_semantics=("arbitrary",), vmem_limit_bytes=VMEM_LIMIT),
        name="proj",
    )(x, norm_w, w_main, w_gate, cos_t, sin_t, conv_w, conv_b)


def _moba_kernel(slopes_ref, qT_ref, k_ref, vT_ref, kmean_ref, g_ref, o_ref,
                 qm_scr, bias_scr, sel_scr, m_scr, l_scr, acc_scr, s_scr):
    p = pl.program_id(0)
    j = pl.program_id(1)
    blk = MOBA_BLOCK
    nb = kmean_ref.shape[0]

    qT2 = qT_ref[0]
    dim_row = lax.broadcasted_iota(jnp.int32, (LANES, blk), 0)
    blk_row = lax.broadcasted_iota(jnp.int32, (nb, blk), 0)
    key_off = lax.broadcasted_iota(jnp.int32, (blk, blk), 0)
    qry_off = lax.broadcasted_iota(jnp.int32, (blk, blk), 1)
    kmean = kmean_ref[...].astype(BF16)
    k_own = k_ref[pl.ds(pl.multiple_of(j * blk, blk), blk), :]
    vT_own = vT_ref[j]

    for hh in range(2):
        slope = slopes_ref[2 * p + hh]
        qm = jnp.where((dim_row // HEAD_DIM) == hh, qT2, jnp.zeros_like(qT2))
        qm_scr[hh] = qm
        bias = slope * key_off.astype(F32)
        bias_scr[hh] = bias

        gate = jnp.dot(kmean, qm, preferred_element_type=F32)
        work = jnp.where(blk_row < j, gate, NEG_INF)
        sel = jnp.zeros((nb, blk), F32)
        for r in range(MOBA_TOPK):
            mx = jnp.max(work, axis=0, keepdims=True)
            idx = jnp.min(jnp.where(work == mx, blk_row, nb), axis=0, keepdims=True)
            pick = blk_row == idx
            sel = jnp.where(jnp.logical_and(pick, r < j), 1.0, sel)
            work = jnp.where(pick, NEG_INF, work)
        sel_scr[hh] = sel

        s = jnp.dot(k_own, qm, preferred_element_type=F32) + bias
        s = jnp.where(key_off <= qry_off, s, NEG_INF)
        m = jnp.max(s, axis=0, keepdims=True)
        pm = jnp.exp2(s - m)
        m_scr[hh] = m
        l_scr[hh] = jnp.sum(pm, axis=0, keepdims=True)
        acc_scr[hh] = jnp.dot(vT_own[hh * HEAD_DIM:(hh + 1) * HEAD_DIM, :], pm.astype(BF16),
                              preferred_element_type=F32)

    def group(g, carry):
        parts = [[], []]
        for b in range(MOBA_GROUP):
            n = g * MOBA_GROUP + b
            k_blk = k_ref[pl.ds(pl.multiple_of(n * blk, blk), blk), :]
            for hh in range(2):
                s_scr[2 * b + hh] = jnp.dot(k_blk, qm_scr[hh], preferred_element_type=F32) + bias_scr[hh]
        for b in range(MOBA_GROUP):
            n = g * MOBA_GROUP + b
            vT_blk = vT_ref[n]
            dist = ((n - j) * blk).astype(F32)
            for hh in range(2):
                s = s_scr[2 * b + hh]
                m_loc = jnp.max(s, axis=0, keepdims=True)
                pm = jnp.exp2(s - m_loc)
                l_loc = jnp.sum(pm, axis=0, keepdims=True)
                o_loc = jnp.dot(vT_blk[hh * HEAD_DIM:(hh + 1) * HEAD_DIM, :], pm.astype(BF16),
                                preferred_element_type=F32)
                chosen = sel_scr[hh, pl.ds(n, 1), :] > 0.0
                cand = jnp.where(chosen, m_loc + slopes_ref[2 * p + hh] * dist, NEG_INF)
                parts[hh].append((cand, l_loc, o_loc))
        for hh in range(2):
            m_old = m_scr[hh]
            m_new = m_old
            for cand, _, _ in parts[hh]:
                m_new = jnp.maximum(m_new, cand)
            alpha = jnp.exp2(m_old - m_new)
            l_new = alpha * l_scr[hh]
            acc = alpha * acc_scr[hh]
            for cand, l_loc, o_loc in parts[hh]:
                w = jnp.exp2(cand - m_new)
                l_new = l_new + w * l_loc
                acc = acc + w * o_loc
            m_scr[hh] = m_new
            l_scr[hh] = l_new
            acc_scr[hh] = acc
        return carry

    lax.fori_loop(0, (j + MOBA_GROUP - 1) // MOBA_GROUP, group, 0)

    oT = jnp.concatenate([acc_scr[0] / l_scr[0], acc_scr[1] / l_scr[1]], axis=0)
    o_ref[...] = (oT.T * _silu(g_ref[...].astype(F32))).astype(o_ref.dtype)


def _moba(slopes, qT, ka, vT, kmean, ga):
    s = ka.shape[0]
    nb = s // MOBA_BLOCK
    blk = MOBA_BLOCK
    npair = A_HEADS // 2
    return pl.pallas_call(
        _moba_kernel,
        grid=(npair, nb),
        in_specs=[
            pl.BlockSpec(memory_space=pltpu.SMEM),
            pl.BlockSpec((1, LANES, blk), lambda p, j: (j, p, 0)),
            pl.BlockSpec((s, LANES), lambda p, j: (0, p)),
            pl.BlockSpec((nb, LANES, blk), lambda p, j: (0, p, 0)),
            pl.BlockSpec((nb, LANES), lambda p, j: (0, p)),
            pl.BlockSpec((blk, LANES), lambda p, j: (j, p)),
        ],
        out_specs=pl.BlockSpec((blk, LANES), lambda p, j: (j, p)),
        out_shape=jax.ShapeDtypeStruct((s, A_WIDTH), BF16),
        scratch_shapes=[
            pltpu.VMEM((2, LANES, blk), BF16),
            pltpu.VMEM((2, blk, blk), F32),
            pltpu.VMEM((2, nb, blk), F32),
            pltpu.VMEM((2, 1, blk), F32),
            pltpu.VMEM((2, 1, blk), F32),
            pltpu.VMEM((2, HEAD_DIM, blk), F32),
            pltpu.VMEM((2 * MOBA_GROUP, blk, blk), F32),
        ],
        compiler_params=pltpu.CompilerParams(
            dimension_semantics=("arbitrary", "arbitrary"), vmem_limit_bytes=VMEM_LIMIT),
        name="moba",
    )(slopes, qT, ka, vT, kmean, ga)


def _head_norm_pair(y, lane_head):
    mu = jnp.zeros_like(y)
    for hh in range(2):
        in_h = lane_head == hh
        mu_h = jnp.sum(jnp.where(in_h, y, 0.0), axis=-1, keepdims=True) * (1.0 / HEAD_DIM)
        mu = jnp.where(in_h, mu_h, mu)
    d = y - mu
    var = jnp.zeros_like(y)
    for hh in range(2):
        in_h = lane_head == hh
        var_h = jnp.sum(jnp.where(in_h, d * d, 0.0), axis=-1, keepdims=True) * (1.0 / HEAD_DIM)
        var = jnp.where(in_h, var_h, var)
    return d * lax.rsqrt(var + HEAD_NORM_EPS)


def _ret_kernel(x_ref, decay_ref, zeta_ref, xi_ref, cd_ref, nw_ref, o_ref, state_scr):
    c = pl.program_id(0)
    L = RET_CHUNK

    @pl.when(c == 0)
    def _():
        state_scr[...] = jnp.zeros_like(state_scr)

    lane_head = lax.broadcasted_iota(jnp.int32, (L, LANES), 1) // HEAD_DIM
    sq_row = lax.broadcasted_iota(jnp.int32, (LANES, LANES), 0) // HEAD_DIM
    sq_col = lax.broadcasted_iota(jnp.int32, (LANES, LANES), 1) // HEAD_DIM
    for t in range(B_WIDTH // LANES):
        cols = slice(t * LANES, (t + 1) * LANES)
        q2 = x_ref[:, t * LANES:(t + 1) * LANES]
        k2 = x_ref[:, B_WIDTH + t * LANES:B_WIDTH + (t + 1) * LANES]
        v2 = x_ref[:, 2 * B_WIDTH + t * LANES:2 * B_WIDTH + (t + 1) * LANES]
        g2 = x_ref[:, 3 * B_WIDTH + t * LANES:3 * B_WIDTH + (t + 1) * LANES]
        state = state_scr[t]
        y = jnp.dot(q2, state.astype(BF16), preferred_element_type=F32) * xi_ref[:, cols]
        for hh in range(2):
            qm = jnp.where(lane_head == hh, q2, jnp.zeros_like(q2))
            sc = lax.dot_general(qm, k2, (((1,), (1,)), ((), ())), preferred_element_type=F32)
            sc = sc * decay_ref[2 * t + hh]
            o_h = jnp.dot(sc.astype(BF16), v2, preferred_element_type=F32)
            y = y + jnp.where(lane_head == hh, o_h, 0.0)
        kz = (k2.astype(F32) * zeta_ref[:, cols]).astype(BF16)
        kv = lax.dot_general(kz, v2, (((0,), (0,)), ((), ())), preferred_element_type=F32)
        state_scr[t] = cd_ref[t] * state + jnp.where(sq_row == sq_col, kv, 0.0)
        yn = _head_norm_pair(y, lane_head) * nw_ref[:, cols]
        o_ref[:, cols] = (yn * _silu(g2.astype(F32))).astype(o_ref.dtype)


def _ret(ret_in, decay, zeta, xi, cd, norm_w):
    s = ret_in.shape[0]
    L = RET_CHUNK
    full = lambda shape: pl.BlockSpec(shape, lambda c: (0,) * len(shape))
    return pl.pallas_call(
        _ret_kernel,
        grid=(s // L,),
        in_specs=[
            pl.BlockSpec((L, 4 * B_WIDTH), lambda c: (c, 0)),
            full((B_HEADS, L, L)),
            full((L, B_WIDTH)),
            full((L, B_WIDTH)),
            full((B_WIDTH // LANES, LANES, LANES)),
            full((1, B_WIDTH)),
        ],
        out_specs=pl.BlockSpec((L, B_WIDTH), lambda c: (c, 0)),
        out_shape=jax.ShapeDtypeStruct((s, B_WIDTH), BF16),
        scratch_shapes=[pltpu.VMEM((B_WIDTH // LANES, LANES, LANES), F32)],
        compiler_params=pltpu.CompilerParams(
            dimension_semantics=("arbitrary",), vmem_limit_bytes=VMEM_LIMIT),
        name="ret",
    )(ret_in, decay, zeta, xi, cd, norm_w)


def _mlstm_kernel(x_ref, gates_ref, gb_ref, nw_ref, o_ref, c_scr, n_scr, m_scr):
    ci = pl.program_id(0)
    L = MLSTM_CHUNK

    @pl.when(ci == 0)
    def _():
        c_scr[...] = jnp.zeros_like(c_scr)
        n_scr[...] = jnp.zeros_like(n_scr)
        m_scr[...] = jnp.zeros_like(m_scr)

    lane_head = lax.broadcasted_iota(jnp.int32, (L, LANES), 1) // HEAD_DIM
    sq_row = lax.broadcasted_iota(jnp.int32, (LANES, LANES), 0) // HEAD_DIM
    sq_col = lax.broadcasted_iota(jnp.int32, (LANES, LANES), 1) // HEAD_DIM
    t_row = lax.broadcasted_iota(jnp.int32, (L, L), 0)
    t_col = lax.broadcasted_iota(jnp.int32, (L, L), 1)
    causal = t_row >= t_col
    lower = causal.astype(F32)

    pre = gates_ref[...] + gb_ref[...]
    logsig = jnp.minimum(pre, 0.0) - jnp.log1p(jnp.exp(-jnp.abs(pre)))
    bsum = jnp.dot(lower, logsig, preferred_element_type=F32, precision=lax.Precision.HIGHEST)
    pre_t = pre.T
    bsum_t = bsum.T

    for t in range(C_WIDTH // LANES):
        cols = slice(t * LANES, (t + 1) * LANES)
        q2 = x_ref[:, t * LANES:(t + 1) * LANES]
        k2 = x_ref[:, C_WIDTH + t * LANES:C_WIDTH + (t + 1) * LANES]
        v2 = x_ref[:, 2 * C_WIDTH + t * LANES:2 * C_WIDTH + (t + 1) * LANES]
        o2 = x_ref[:, 3 * C_WIDTH + t * LANES:3 * C_WIDTH + (t + 1) * LANES]
        g2 = x_ref[:, 4 * C_WIDTH + t * LANES:4 * C_WIDTH + (t + 1) * LANES]
        c_st = c_scr[t]
        n_st = n_scr[t]
        qc = jnp.dot(q2, c_st.astype(BF16), preferred_element_type=F32)
        qn = q2.astype(F32) * n_st
        k2f = k2.astype(F32)

        num = jnp.zeros((L, LANES), F32)
        den = jnp.zeros((L, LANES), F32)
        emm = jnp.zeros((L, LANES), F32)
        a_lane = jnp.zeros((1, LANES), F32)
        w_lane = jnp.zeros((L, LANES), F32)
        m_lane = jnp.zeros((1, LANES), F32)
        for hh in range(2):
            h = 2 * t + hh
            in_h = lane_head == hh
            i_col = pre[:, h:h + 1]
            b_col = bsum[:, C_HEADS + h:C_HEADS + h + 1]
            i_row = pre_t[h:h + 1, :]
            b_row = bsum_t[C_HEADS + h:C_HEADS + h + 1, :]
            m_st = m_scr[t, :, hh * HEAD_DIM:hh * HEAD_DIM + 1]

            d_log = jnp.where(causal, b_col - b_row + i_row, NEG_INF)
            inter_log = b_col + m_st
            m_t = jnp.maximum(inter_log, jnp.max(d_log, axis=-1, keepdims=True))
            w_intra = jnp.exp(d_log - m_t)
            w_inter = jnp.exp(inter_log - m_t)
            qm = jnp.where(in_h, q2, jnp.zeros_like(q2))
            sc = lax.dot_general(qm, k2, (((1,), (1,)), ((), ())), preferred_element_type=F32) * w_intra
            num_h = jnp.dot(sc.astype(BF16), v2, preferred_element_type=F32) + w_inter * qc
            den_h = (jnp.sum(sc, axis=-1, keepdims=True)
                     + w_inter * jnp.sum(jnp.where(in_h, qn, 0.0), axis=-1, keepdims=True))
            num = jnp.where(in_h, num_h, num)
            den = jnp.where(in_h, den_h, den)
            emm = jnp.where(in_h, jnp.exp(-m_t), emm)

            b_last = b_col[L - 1:L, :]
            g_col = b_last - b_col + i_col
            m_new = jnp.maximum(b_last + m_st, jnp.max(g_col, axis=0, keepdims=True))
            a = jnp.exp(b_last + m_st - m_new)
            w_col = jnp.exp(g_col - m_new)
            in_h1 = in_h[0:1, :]
            a_lane = jnp.where(in_h1, a, a_lane)
            m_lane = jnp.where(in_h1, m_new, m_lane)
            w_lane = jnp.where(in_h, w_col, w_lane)

        h_t = num / jnp.maximum(jnp.abs(den), emm)
        kw = k2f * w_lane
        kv = lax.dot_general(kw.astype(BF16), v2, (((0,), (0,)), ((), ())), preferred_element_type=F32)
        a_col = jnp.transpose(jnp.broadcast_to(a_lane, (8, LANES)))[:, 0:1]
        c_scr[t] = a_col * c_st + jnp.where(sq_row == sq_col, kv, 0.0)
        n_scr[t] = a_lane * n_st + jnp.sum(kw, axis=0, keepdims=True)
        m_scr[t] = m_lane

        y = h_t * jax.nn.sigmoid(o2.astype(F32))
        yn = _head_norm_pair(y, lane_head) * nw_ref[:, cols]
        o_ref[:, cols] = (yn * _silu(g2.astype(F32))).astype(o_ref.dtype)


def _mlstm(mls_in, gates, gate_bias, norm_w):
    s = mls_in.shape[0]
    L = MLSTM_CHUNK
    npair = C_WIDTH // LANES
    full = lambda shape: pl.BlockSpec(shape, lambda c: (0,) * len(shape))
    return pl.pallas_call(
        _mlstm_kernel,
        grid=(s // L,),
        in_specs=[
            pl.BlockSpec((L, 5 * C_WIDTH), lambda c: (c, 0)),
            pl.BlockSpec((L, LANES), lambda c: (c, 0)),
            full((1, LANES)),
            full((1, C_WIDTH)),
        ],
        out_specs=pl.BlockSpec((L, C_WIDTH), lambda c: (c, 0)),
        out_shape=jax.ShapeDtypeStruct((s, C_WIDTH), BF16),
        scratch_shapes=[
            pltpu.VMEM((npair, LANES, LANES), F32),
            pltpu.VMEM((npair, 1, LANES), F32),
            pltpu.VMEM((npair, 1, LANES), F32),
        ],
        compiler_params=pltpu.CompilerParams(
            dimension_semantics=("arbitrary",), vmem_limit_bytes=VMEM_LIMIT),
        name="mlstm",
    )(mls_in, gates, gate_bias, norm_w)


def _outproj_kernel(x_ref, ya_ref, yb_ref, yc_ref, wa_ref, wb_ref, wc_ref, fw_ref, o_ref, *, final_norm):
    acc = x_ref[...]
    acc = acc + jnp.dot(ya_ref[...], wa_ref[...], preferred_element_type=F32)
    acc = acc + jnp.dot(yb_ref[...], wb_ref[...], preferred_element_type=F32)
    acc = acc + jnp.dot(yc_ref[...], wc_ref[...], preferred_element_type=F32)
    if final_norm:
        ms = jnp.mean(acc * acc, axis=-1, keepdims=True)
        acc = (acc * lax.rsqrt(ms + RMS_EPS)) * fw_ref[...]
    o_ref[...] = acc


def _outproj(x, ya, yb, yc, wa, wb, wc, final_w, final_norm):
    s = x.shape[0]
    tm = OUT_ROWS
    full = lambda shape: pl.BlockSpec(shape, lambda i: (0,) * len(shape))
    rows = lambda width: pl.BlockSpec((tm, width), lambda i: (i, 0))
    return pl.pallas_call(
        functools.partial(_outproj_kernel, final_norm=final_norm),
        grid=(s // tm,),
        in_specs=[rows(D_MODEL), rows(A_WIDTH), rows(B_WIDTH), rows(C_WIDTH),
                  full((A_WIDTH, D_MODEL)), full((B_WIDTH, D_MODEL)), full((C_WIDTH, D_MODEL)),
                  full((1, D_MODEL))],
        out_specs=rows(D_MODEL),
        out_shape=jax.ShapeDtypeStruct((s, D_MODEL), F32),
        compiler_params=pltpu.CompilerParams(
            dimension_semantics=("arbitrary",), vmem_limit_bytes=VMEM_LIMIT),
        name="outproj",
    )(x, ya, yb, yc, wa, wb, wc, final_w)


def _rope_tables(s):
    half = HEAD_DIM // 2
    pos = jnp.arange(s, dtype=F32)
    theta = 1.0 / (ROPE_BASE ** jnp.linspace(0.0, 1.0, half, dtype=F32))
    ang = pos[:, None] * theta[None, :]
    cos, sin = jnp.cos(ang), jnp.sin(ang)
    cos_t = jnp.tile(cos, (1, LANES // half))
    sin_t = jnp.tile(jnp.concatenate([-sin, sin], axis=-1), (1, LANES // HEAD_DIM))
    return cos_t, sin_t


def _retention_tables():
    L = RET_CHUNK
    h = B_HEADS
    log_gamma = jnp.log(1.0 - jnp.exp2(-5.0 - jnp.arange(h, dtype=F32)))
    pos = jnp.arange(L, dtype=F32)
    diff = pos[:, None] - pos[None, :]
    decay = jnp.where(diff >= 0, jnp.exp(log_gamma[:, None, None] * jnp.maximum(diff, 0.0)), 0.0)
    zeta = jnp.exp(log_gamma[:, None] * (L - 1.0 - pos))
    xi = jnp.exp(log_gamma[:, None] * (pos + 1.0))
    chunk_decay = jnp.exp(log_gamma * L)
    per_lane = lambda t: jnp.repeat(t.T, HEAD_DIM, axis=1)
    cd_lane = jnp.repeat(chunk_decay, HEAD_DIM).reshape(B_WIDTH // LANES, LANES, 1)
    cd = jnp.broadcast_to(cd_lane, (B_WIDTH // LANES, LANES, LANES))
    return decay, per_lane(zeta), per_lane(xi), cd


def _alibi_slopes():
    return jnp.exp2(-8.0 * jnp.arange(1, A_HEADS + 1, dtype=F32) / A_HEADS) * LOG2E


def kernel(x, norm_w, w_in, conv_w, conv_b, gate_bias, ret_norm_w, mlstm_norm_w, w_out, final_norm_w):
    b, s, d = x.shape
    assert b == 1 and d == D_MODEL and s % PROJ_ROWS == 0 and s % OUT_ROWS == 0
    depth = w_in.shape[0]
    cos_t, sin_t = _rope_tables(s)
    decay, zeta, xi, cd = _retention_tables()
    slopes = _alibi_slopes()

    w_main = w_in[:, :, :N_MAIN].astype(BF16)
    w_gate = jnp.pad(w_in[:, :, N_MAIN:], ((0, 0), (0, 0), (0, LANES - 2 * C_HEADS)))
    w_gate = w_gate.astype(BF16)
    w_out_b = w_out.astype(BF16)
    gb = jnp.pad(gate_bias.astype(F32), ((0, 0), (0, LANES - 2 * C_HEADS)))

    xs = x[0]
    for layer in range(depth):
        qT, ka, vT, ga, kmean, ret_in, mls_in, gates = _proj(
            xs, norm_w[layer][None, :], w_main[layer], w_gate[layer], cos_t, sin_t,
            conv_w[layer], conv_b[layer][None, :])
        ya = _moba(slopes, qT, ka, vT, kmean.reshape(-1, A_WIDTH), ga)
        yb = _ret(ret_in, decay, zeta, xi, cd, ret_norm_w[layer][None, :])
        yc = _mlstm(mls_in, gates, gb[layer][None, :], mlstm_norm_w[layer][None, :])
        wo = w_out_b[layer]
        xs = _outproj(xs, ya, yb, yc, wo[:A_WIDTH], wo[A_WIDTH:A_WIDTH + B_WIDTH], wo[A_WIDTH + B_WIDTH:],
                      final_norm_w[None, :], final_norm=(layer == depth - 1))
    return xs[None]
```

```python
import functools

import numpy as np
import jax
import jax.numpy as jnp
from jax import lax
from jax.experimental import pallas as pl
from jax.experimental.pallas import tpu as pltpu

D_MODEL = 1024
HEAD_DIM = 64
A_HEADS, B_HEADS, C_HEADS = 8, 4, 4
A_WIDTH, B_WIDTH, C_WIDTH = A_HEADS * HEAD_DIM, B_HEADS * HEAD_DIM, C_HEADS * HEAD_DIM
MOBA_BLOCK = 256
MOBA_TOPK = 3
MOBA_GROUP = 4
MOBA_BIG_GROUP = 16
RET_CHUNK = 256
MLSTM_CHUNK = 128
CONV_K = 4
RMS_EPS = 1e-6
HEAD_NORM_EPS = 1e-5
ROPE_BASE = 10000.0
N_MAIN = 4 * A_WIDTH + 4 * B_WIDTH + 5 * C_WIDTH
QK_SCALE = HEAD_DIM ** -0.5
LOG2E = 1.4426950408889634

LANES = 128
CONV_PAD = 8
PROJ_ROWS = 512
OUT_ROWS = 512
VMEM_LIMIT = 56 * 1024 * 1024

BF16 = jnp.bfloat16
F32 = jnp.float32
NEG_INF = float("-inf")


def _silu(t):
    return t * jax.nn.sigmoid(t)


def _swap_half_heads(t):
    lane = lax.broadcasted_iota(jnp.int32, t.shape, 1)
    first_half = (lane % HEAD_DIM) < (HEAD_DIM // 2)
    return jnp.where(first_half, pltpu.roll(t, LANES - HEAD_DIM // 2, 1), pltpu.roll(t, HEAD_DIM // 2, 1))


def _proj_kernel(x_ref, nw_ref, w_ref, wg_ref, cos_ref, sin_ref, cw_ref, cb_ref,
                 qT_ref, ka_ref, vT_ref, ga_ref, kmean_ref, ret_ref, mls_ref, gates_ref,
                 h_scr, ext_scr):
    i = pl.program_id(0)
    tm = x_ref.shape[0]
    nblk = tm // MOBA_BLOCK

    x = x_ref[...]
    ms = jnp.mean(x * x, axis=-1, keepdims=True)
    h_scr[...] = ((x * lax.rsqrt(ms + RMS_EPS)) * nw_ref[...]).astype(BF16)

    def seg(lo, width):
        return jnp.dot(h_scr[...], w_ref[:, lo:lo + width], preferred_element_type=F32)

    aq = seg(0, A_WIDTH) * (QK_SCALE * LOG2E)
    ak = seg(A_WIDTH, A_WIDTH)
    av = seg(2 * A_WIDTH, A_WIDTH)
    ka_ref[...] = ak.astype(BF16)
    ga_ref[...] = seg(3 * A_WIDTH, A_WIDTH).astype(BF16)
    for r in range(nblk):
        rows = slice(r * MOBA_BLOCK, (r + 1) * MOBA_BLOCK)
        qT_ref[r] = aq[rows].T.astype(BF16)
        vT_ref[r] = av[rows].T.astype(BF16)
        kmean_ref[r] = jnp.mean(ak[rows], axis=0, keepdims=True)

    b0 = 4 * A_WIDTH
    cos = cos_ref[...]
    sin = sin_ref[...]
    for t in range(2 * B_WIDTH // LANES):
        v = seg(b0 + t * LANES, LANES)
        rot = v * cos + _swap_half_heads(v) * sin
        if t >= B_WIDTH // LANES:
            rot = rot * QK_SCALE
        ret_ref[:, t * LANES:(t + 1) * LANES] = rot.astype(BF16)
    ret_ref[:, 2 * B_WIDTH:4 * B_WIDTH] = seg(b0 + 2 * B_WIDTH, 2 * B_WIDTH).astype(BF16)

    c0 = b0 + 4 * B_WIDTH

    @pl.when(i == 0)
    def _():
        ext_scr[0:CONV_PAD, :] = jnp.zeros((CONV_PAD, 2 * C_WIDTH), F32)

    ext_scr[CONV_PAD:CONV_PAD + tm, :] = seg(c0, 2 * C_WIDTH)
    y = jnp.zeros((tm, 2 * C_WIDTH), F32) + cb_ref[...]
    for tap in range(CONV_K):
        lo = CONV_PAD - (CONV_K - 1) + tap
        y = y + cw_ref[tap:tap + 1, :] * ext_scr[lo:lo + tm, :]
    ext_scr[0:CONV_PAD, :] = ext_scr[tm:tm + CONV_PAD, :]
    qk = _silu(y)
    mls_ref[:, 0:C_WIDTH] = qk[:, 0:C_WIDTH].astype(BF16)
    mls_ref[:, C_WIDTH:2 * C_WIDTH] = (qk[:, C_WIDTH:] * QK_SCALE).astype(BF16)
    mls_ref[:, 2 * C_WIDTH:5 * C_WIDTH] = seg(c0 + 2 * C_WIDTH, 3 * C_WIDTH).astype(BF16)

    gates_ref[...] = jnp.dot(h_scr[...], wg_ref[...], preferred_element_type=F32)


def _proj(x, norm_w, w_main, w_gate, cos_t, sin_t, conv_w, conv_b):
    s = x.shape[0]
    tm = PROJ_ROWS
    nblk = tm // MOBA_BLOCK
    nb = s // MOBA_BLOCK
    full = lambda shape: pl.BlockSpec(shape, lambda i: (0,) * len(shape))
    return pl.pallas_call(
        _proj_kernel,
        grid=(s // tm,),
        in_specs=[
            pl.BlockSpec((tm, D_MODEL), lambda i: (i, 0)),
            full((1, D_MODEL)),
            full((D_MODEL, N_MAIN)),
            full((D_MODEL, LANES)),
            pl.BlockSpec((tm, LANES), lambda i: (i, 0)),
            pl.BlockSpec((tm, LANES), lambda i: (i, 0)),
            full((CONV_K, 2 * C_WIDTH)),
            full((1, 2 * C_WIDTH)),
        ],
        out_specs=[
            pl.BlockSpec((nblk, A_WIDTH, MOBA_BLOCK), lambda i: (i, 0, 0)),
            pl.BlockSpec((tm, A_WIDTH), lambda i: (i, 0)),
            pl.BlockSpec((nblk, A_WIDTH, MOBA_BLOCK), lambda i: (i, 0, 0)),
            pl.BlockSpec((tm, A_WIDTH), lambda i: (i, 0)),
            pl.BlockSpec((nblk, 1, A_WIDTH), lambda i: (i, 0, 0)),
            pl.BlockSpec((tm, 4 * B_WIDTH), lambda i: (i, 0)),
            pl.BlockSpec((tm, 5 * C_WIDTH), lambda i: (i, 0)),
            pl.BlockSpec((tm, LANES), lambda i: (i, 0)),
        ],
        out_shape=[
            jax.ShapeDtypeStruct((nb, A_WIDTH, MOBA_BLOCK), BF16),
            jax.ShapeDtypeStruct((s, A_WIDTH), BF16),
            jax.ShapeDtypeStruct((nb, A_WIDTH, MOBA_BLOCK), BF16),
            jax.ShapeDtypeStruct((s, A_WIDTH), BF16),
            jax.ShapeDtypeStruct((nb, 1, A_WIDTH), F32),
            jax.ShapeDtypeStruct((s, 4 * B_WIDTH), BF16),
            jax.ShapeDtypeStruct((s, 5 * C_WIDTH), BF16),
            jax.ShapeDtypeStruct((s, LANES), F32),
        ],
        scratch_shapes=[
            pltpu.VMEM((tm, D_MODEL), BF16),
            pltpu.VMEM((tm + CONV_PAD, 2 * C_WIDTH), F32),
        ],
        compiler_params=pltpu.CompilerParams(
            dimension_semantics=("arbitrary",), vmem_limit_bytes=VMEM_LIMIT),
        name="proj",
    )(x, norm_w, w_main, w_gate, cos_t, sin_t, conv_w, conv_b)


def _moba_kernel(slopes_ref, qT_ref, k_ref, vT_ref, kmean_ref, g_ref, o_ref,
                 qm_scr, bias_scr, sel_scr, m_scr, l_scr, acc_scr, s_scr, gate_scr, own_scr):
    p = pl.program_id(0)
    j = pl.program_id(1)
    blk = MOBA_BLOCK
    nb = kmean_ref.shape[0]

    qT2 = qT_ref[0]
    dim_row = lax.broadcasted_iota(jnp.int32, (LANES, blk), 0)
    blk_row = lax.broadcasted_iota(jnp.int32, (nb, blk), 0)
    key_off = lax.broadcasted_iota(jnp.int32, (blk, blk), 0)
    qry_off = lax.broadcasted_iota(jnp.int32, (blk, blk), 1)
    kmean = kmean_ref[...].astype(BF16)
    k_own = k_ref[pl.ds(pl.multiple_of(j * blk, blk), blk), :]
    vT_own = vT_ref[j]

    for hh in range(2):
        qm = jnp.where((dim_row // HEAD_DIM) == hh, qT2, jnp.zeros_like(qT2))
        qm_scr[hh] = qm
        bias_scr[hh] = slopes_ref[2 * p + hh] * key_off.astype(F32)
        gate_scr[hh] = jnp.dot(kmean, qm, preferred_element_type=F32)
        own_scr[hh] = jnp.dot(k_own, qm, preferred_element_type=F32)

    for hh in range(2):
        work = jnp.where(blk_row < j, gate_scr[hh], NEG_INF)
        sel = jnp.zeros((nb, blk), F32)
        for r in range(MOBA_TOPK):
            mx = jnp.max(work, axis=0, keepdims=True)
            idx = jnp.min(jnp.where(work == mx, blk_row, nb), axis=0, keepdims=True)
            pick = blk_row == idx
            sel = jnp.where(jnp.logical_and(pick, r < j), 1.0, sel)
            work = jnp.where(pick, NEG_INF, work)
        sel_scr[hh] = sel

        s = jnp.where(key_off <= qry_off, own_scr[hh] + bias_scr[hh], NEG_INF)
        m = jnp.max(s, axis=0, keepdims=True)
        pm = jnp.exp2(s - m)
        m_scr[hh] = m
        l_scr[hh] = jnp.sum(pm, axis=0, keepdims=True)
        acc_scr[hh] = jnp.dot(vT_own[hh * HEAD_DIM:(hh + 1) * HEAD_DIM, :], pm.astype(BF16),
                              preferred_element_type=F32)

    def make_group(size, first_block):
        def group(g, carry):
            base = first_block + g * size
            for b in range(size):
                k_blk = k_ref[pl.ds(pl.multiple_of((base + b) * blk, blk), blk), :]
                for hh in range(2):
                    s_scr[2 * b + hh] = jnp.dot(k_blk, qm_scr[hh], preferred_element_type=F32)
            parts = [[], []]
            for b in range(size):
                n = base + b
                vT_blk = vT_ref[n]
                dist = ((n - j) * blk).astype(F32)
                for hh in range(2):
                    s = s_scr[2 * b + hh] + bias_scr[hh]
                    m_loc = jnp.max(s, axis=0, keepdims=True)
                    pm = jnp.exp2(s - m_loc)
                    l_loc = jnp.sum(pm, axis=0, keepdims=True)
                    o_loc = jnp.dot(vT_blk[hh * HEAD_DIM:(hh + 1) * HEAD_DIM, :], pm.astype(BF16),
                                    preferred_element_type=F32)
                    chosen = sel_scr[hh, pl.ds(n, 1), :] > 0.0
                    cand = jnp.where(chosen, m_loc + slopes_ref[2 * p + hh] * dist, NEG_INF)
                    parts[hh].append((cand, l_loc, o_loc))
            for hh in range(2):
                m_old = m_scr[hh]
                m_new = m_old
                for cand, _, _ in parts[hh]:
                    m_new = jnp.maximum(m_new, cand)
                alpha = jnp.exp2(m_old - m_new)
                l_new = alpha * l_scr[hh]
                acc = alpha * acc_scr[hh]
                for cand, l_loc, o_loc in parts[hh]:
                    w = jnp.exp2(cand - m_new)
                    l_new = l_new + w * l_loc
                    acc = acc + w * o_loc
                m_scr[hh] = m_new
                l_scr[hh] = l_new
                acc_scr[hh] = acc
            return carry
        return group

    n_big = j // MOBA_BIG_GROUP
    done = n_big * MOBA_BIG_GROUP
    lax.fori_loop(0, n_big, make_group(MOBA_BIG_GROUP, 0), 0)
    lax.fori_loop(0, (j - done + MOBA_GROUP - 1) // MOBA_GROUP, make_group(MOBA_GROUP, done), 0)

    oT = jnp.concatenate([acc_scr[0] / l_scr[0], acc_scr[1] / l_scr[1]], axis=0)
    o_ref[...] = (oT.T * _silu(g_ref[...].astype(F32))).astype(o_ref.dtype)


def _moba(slopes, qT, ka, vT, kmean, ga):
    s = ka.shape[0]
    nb = s // MOBA_BLOCK
    blk = MOBA_BLOCK
    npair = A_HEADS // 2
    return pl.pallas_call(
        _moba_kernel,
        grid=(npair, nb),
        in_specs=[
            pl.BlockSpec(memory_space=pltpu.SMEM),
            pl.BlockSpec((1, LANES, blk), lambda p, j: (j, p, 0)),
            pl.BlockSpec((s, LANES), lambda p, j: (0, p)),
            pl.BlockSpec((nb, LANES, blk), lambda p, j: (0, p, 0)),
            pl.BlockSpec((nb, LANES), lambda p, j: (0, p)),
            pl.BlockSpec((blk, LANES), lambda p, j: (j, p)),
        ],
        out_specs=pl.BlockSpec((blk, LANES), lambda p, j: (j, p)),
        out_shape=jax.ShapeDtypeStruct((s, A_WIDTH), BF16),
        scratch_shapes=[
            pltpu.VMEM((2, LANES, blk), BF16),
            pltpu.VMEM((2, blk, blk), F32),
            pltpu.VMEM((2, nb, blk), F32),
            pltpu.VMEM((2, 1, blk), F32),
            pltpu.VMEM((2, 1, blk), F32),
            pltpu.VMEM((2, HEAD_DIM, blk), F32),
            pltpu.VMEM((2 * MOBA_BIG_GROUP, blk, blk), F32),
            pltpu.VMEM((2, nb, blk), F32),
            pltpu.VMEM((2, blk, blk), F32),
        ],
        compiler_params=pltpu.CompilerParams(
            dimension_semantics=("arbitrary", "arbitrary"), vmem_limit_bytes=VMEM_LIMIT),
        name="moba",
    )(slopes, qT, ka, vT, kmean, ga)


def _head_norm_pair(y, lane_head):
    mu = jnp.zeros_like(y)
    for hh in range(2):
        in_h = lane_head == hh
        mu_h = jnp.sum(jnp.where(in_h, y, 0.0), axis=-1, keepdims=True) * (1.0 / HEAD_DIM)
        mu = jnp.where(in_h, mu_h, mu)
    d = y - mu
    var = jnp.zeros_like(y)
    for hh in range(2):
        in_h = lane_head == hh
        var_h = jnp.sum(jnp.where(in_h, d * d, 0.0), axis=-1, keepdims=True) * (1.0 / HEAD_DIM)
        var = jnp.where(in_h, var_h, var)
    return d * lax.rsqrt(var + HEAD_NORM_EPS)


def _ret_kernel(x_ref, decay_ref, zeta_ref, xi_ref, cd_ref, nw_ref, o_ref, state_scr):
    c = pl.program_id(0)
    L = RET_CHUNK

    @pl.when(c == 0)
    def _():
        state_scr[...] = jnp.zeros_like(state_scr)

    lane_head = lax.broadcasted_iota(jnp.int32, (L, LANES), 1) // HEAD_DIM
    sq_row = lax.broadcasted_iota(jnp.int32, (LANES, LANES), 0) // HEAD_DIM
    sq_col = lax.broadcasted_iota(jnp.int32, (LANES, LANES), 1) // HEAD_DIM
    for t in range(B_WIDTH // LANES):
        cols = slice(t * LANES, (t + 1) * LANES)
        q2 = x_ref[:, t * LANES:(t + 1) * LANES]
        k2 = x_ref[:, B_WIDTH + t * LANES:B_WIDTH + (t + 1) * LANES]
        v2 = x_ref[:, 2 * B_WIDTH + t * LANES:2 * B_WIDTH + (t + 1) * LANES]
        g2 = x_ref[:, 3 * B_WIDTH + t * LANES:3 * B_WIDTH + (t + 1) * LANES]
        state = state_scr[t]
        y = jnp.dot(q2, state.astype(BF16), preferred_element_type=F32) * xi_ref[:, cols]
        for hh in range(2):
            qm = jnp.where(lane_head == hh, q2, jnp.zeros_like(q2))
            sc = lax.dot_general(qm, k2, (((1,), (1,)), ((), ())), preferred_element_type=F32)
            sc = sc * decay_ref[2 * t + hh]
            o_h = jnp.dot(sc.astype(BF16), v2, preferred_element_type=F32)
            y = y + jnp.where(lane_head == hh, o_h, 0.0)
        kz = (k2.astype(F32) * zeta_ref[:, cols]).astype(BF16)
        kv = lax.dot_general(kz, v2, (((0,), (0,)), ((), ())), preferred_element_type=F32)
        state_scr[t] = cd_ref[t] * state + jnp.where(sq_row == sq_col, kv, 0.0)
        yn = _head_norm_pair(y, lane_head) * nw_ref[:, cols]
        o_ref[:, cols] = (yn * _silu(g2.astype(F32))).astype(o_ref.dtype)


def _ret(ret_in, decay, zeta, xi, cd, norm_w):
    s = ret_in.shape[0]
    L = RET_CHUNK
    full = lambda shape: pl.BlockSpec(shape, lambda c: (0,) * len(shape))
    return pl.pallas_call(
        _ret_kernel,
        grid=(s // L,),
        in_specs=[
            pl.BlockSpec((L, 4 * B_WIDTH), lambda c: (c, 0)),
            full((B_HEADS, L, L)),
            full((L, B_WIDTH)),
            full((L, B_WIDTH)),
            full((B_WIDTH // LANES, LANES, LANES)),
            full((1, B_WIDTH)),
        ],
        out_specs=pl.BlockSpec((L, B_WIDTH), lambda c: (c, 0)),
        out_shape=jax.ShapeDtypeStruct((s, B_WIDTH), BF16),
        scratch_shapes=[pltpu.VMEM((B_WIDTH // LANES, LANES, LANES), F32)],
        compiler_params=pltpu.CompilerParams(
            dimension_semantics=("arbitrary",), vmem_limit_bytes=VMEM_LIMIT),
        name="ret",
    )(ret_in, decay, zeta, xi, cd, norm_w)


def _mlstm_kernel(x_ref, gates_ref, gb_ref, nw_ref, o_ref, c_scr, n_scr, m_scr):
    ci = pl.program_id(0)
    L = MLSTM_CHUNK

    @pl.when(ci == 0)
    def _():
        c_scr[...] = jnp.zeros_like(c_scr)
        n_scr[...] = jnp.zeros_like(n_scr)
        m_scr[...] = jnp.zeros_like(m_scr)

    lane_head = lax.broadcasted_iota(jnp.int32, (L, LANES), 1) // HEAD_DIM
    sq_row = lax.broadcasted_iota(jnp.int32, (LANES, LANES), 0) // HEAD_DIM
    sq_col = lax.broadcasted_iota(jnp.int32, (LANES, LANES), 1) // HEAD_DIM
    t_row = lax.broadcasted_iota(jnp.int32, (L, L), 0)
    t_col = lax.broadcasted_iota(jnp.int32, (L, L), 1)
    causal = t_row >= t_col
    lower = causal.astype(F32)

    pre = gates_ref[...] + gb_ref[...]
    logsig = jnp.minimum(pre, 0.0) - jnp.log1p(jnp.exp(-jnp.abs(pre)))
    bsum = jnp.dot(lower, logsig, preferred_element_type=F32, precision=lax.Precision.HIGHEST)
    pre_t = pre.T
    bsum_t = bsum.T

    for t in range(C_WIDTH // LANES):
        cols = slice(t * LANES, (t + 1) * LANES)
        q2 = x_ref[:, t * LANES:(t + 1) * LANES]
        k2 = x_ref[:, C_WIDTH + t * LANES:C_WIDTH + (t + 1) * LANES]
        v2 = x_ref[:, 2 * C_WIDTH + t * LANES:2 * C_WIDTH + (t + 1) * LANES]
        o2 = x_ref[:, 3 * C_WIDTH + t * LANES:3 * C_WIDTH + (t + 1) * LANES]
        g2 = x_ref[:, 4 * C_WIDTH + t * LANES:4 * C_WIDTH + (t + 1) * LANES]
        c_st = c_scr[t]
        n_st = n_scr[t]
        qc = jnp.dot(q2, c_st.astype(BF16), preferred_element_type=F32)
        qn = q2.astype(F32) * n_st
        k2f = k2.astype(F32)

        num = jnp.zeros((L, LANES), F32)
        den = jnp.zeros((L, LANES), F32)
        emm = jnp.zeros((L, LANES), F32)
        a_lane = jnp.zeros((1, LANES), F32)
        w_lane = jnp.zeros((L, LANES), F32)
        m_lane = jnp.zeros((1, LANES), F32)
        for hh in range(2):
            h = 2 * t + hh
            in_h = lane_head == hh
            i_col = pre[:, h:h + 1]
            b_col = bsum[:, C_HEADS + h:C_HEADS + h + 1]
            i_row = pre_t[h:h + 1, :]
            b_row = bsum_t[C_HEADS + h:C_HEADS + h + 1, :]
            m_st = m_scr[t, :, hh * HEAD_DIM:hh * HEAD_DIM + 1]

            d_log = jnp.where(causal, b_col - b_row + i_row, NEG_INF)
            inter_log = b_col + m_st
            m_t = jnp.maximum(inter_log, jnp.max(d_log, axis=-1, keepdims=True))
            w_intra = jnp.exp(d_log - m_t)
            w_inter = jnp.exp(inter_log - m_t)
            qm = jnp.where(in_h, q2, jnp.zeros_like(q2))
            sc = lax.dot_general(qm, k2, (((1,), (1,)), ((), ())), preferred_element_type=F32) * w_intra
            num_h = jnp.dot(sc.astype(BF16), v2, preferred_element_type=F32) + w_inter * qc
            den_h = (jnp.sum(sc, axis=-1, keepdims=True)
                     + w_inter * jnp.sum(jnp.where(in_h, qn, 0.0), axis=-1, keepdims=True))
            num = jnp.where(in_h, num_h, num)
            den = jnp.where(in_h, den_h, den)
            emm = jnp.where(in_h, jnp.exp(-m_t), emm)

            b_last = b_col[L - 1:L, :]
            g_col = b_last - b_col + i_col
            m_new = jnp.maximum(b_last + m_st, jnp.max(g_col, axis=0, keepdims=True))
            a = jnp.exp(b_last + m_st - m_new)
            w_col = jnp.exp(g_col - m_new)
            in_h1 = in_h[0:1, :]
            a_lane = jnp.where(in_h1, a, a_lane)
            m_lane = jnp.where(in_h1, m_new, m_lane)
            w_lane = jnp.where(in_h, w_col, w_lane)

        h_t = num / jnp.maximum(jnp.abs(den), emm)
        kw = k2f * w_lane
        kv = lax.dot_general(kw.astype(BF16), v2, (((0,), (0,)), ((), ())), preferred_element_type=F32)
        a_col = jnp.transpose(jnp.broadcast_to(a_lane, (8, LANES)))[:, 0:1]
        c_scr[t] = a_col * c_st + jnp.where(sq_row == sq_col, kv, 0.0)
        n_scr[t] = a_lane * n_st + jnp.sum(kw, axis=0, keepdims=True)
        m_scr[t] = m_lane

        y = h_t * jax.nn.sigmoid(o2.astype(F32))
        yn = _head_norm_pair(y, lane_head) * nw_ref[:, cols]
        o_ref[:, cols] = (yn * _silu(g2.astype(F32))).astype(o_ref.dtype)


def _mlstm(mls_in, gates, gate_bias, norm_w):
    s = mls_in.shape[0]
    L = MLSTM_CHUNK
    npair = C_WIDTH // LANES
    full = lambda shape: pl.BlockSpec(shape, lambda c: (0,) * len(shape))
    return pl.pallas_call(
        _mlstm_kernel,
        grid=(s // L,),
        in_specs=[
            pl.BlockSpec((L, 5 * C_WIDTH), lambda c: (c, 0)),
            pl.BlockSpec((L, LANES), lambda c: (c, 0)),
            full((1, LANES)),
            full((1, C_WIDTH)),
        ],
        out_specs=pl.BlockSpec((L, C_WIDTH), lambda c: (c, 0)),
        out_shape=jax.ShapeDtypeStruct((s, C_WIDTH), BF16),
        scratch_shapes=[
            pltpu.VMEM((npair, LANES, LANES), F32),
            pltpu.VMEM((npair, 1, LANES), F32),
            pltpu.VMEM((npair, 1, LANES), F32),
        ],
        compiler_params=pltpu.CompilerParams(
            dimension_semantics=("arbitrary",), vmem_limit_bytes=VMEM_LIMIT),
        name="mlstm",
    )(mls_in, gates, gate_bias, norm_w)


def _outproj_kernel(x_ref, ya_ref, yb_ref, yc_ref, wa_ref, wb_ref, wc_ref, fw_ref, o_ref, *, final_norm):
    acc = x_ref[...]
    acc = acc + jnp.dot(ya_ref[...], wa_ref[...], preferred_element_type=F32)
    acc = acc + jnp.dot(yb_ref[...], wb_ref[...], preferred_element_type=F32)
    acc = acc + jnp.dot(yc_ref[...], wc_ref[...], preferred_element_type=F32)
    if final_norm:
        ms = jnp.mean(acc * acc, axis=-1, keepdims=True)
        acc = (acc * lax.rsqrt(ms + RMS_EPS)) * fw_ref[...]
    o_ref[...] = acc


def _outproj(x, ya, yb, yc, wa, wb, wc, final_w, final_norm):
    s = x.shape[0]
    tm = OUT_ROWS
    full = lambda shape: pl.BlockSpec(shape, lambda i: (0,) * len(shape))
    rows = lambda width: pl.BlockSpec((tm, width), lambda i: (i, 0))
    return pl.pallas_call(
        functools.partial(_outproj_kernel, final_norm=final_norm),
        grid=(s // tm,),
        in_specs=[rows(D_MODEL), rows(A_WIDTH), rows(B_WIDTH), rows(C_WIDTH),
                  full((A_WIDTH, D_MODEL)), full((B_WIDTH, D_MODEL)), full((C_WIDTH, D_MODEL)),
                  full((1, D_MODEL))],
        out_specs=rows(D_MODEL),
        out_shape=jax.ShapeDtypeStruct((s, D_MODEL), F32),
        compiler_params=pltpu.CompilerParams(
            dimension_semantics=("arbitrary",), vmem_limit_bytes=VMEM_LIMIT),
        name="outproj",
    )(x, ya, yb, yc, wa, wb, wc, final_w)


def _rope_tables(s):
    half = HEAD_DIM // 2
    pos = jnp.arange(s, dtype=F32)
    theta = 1.0 / (ROPE_BASE ** jnp.linspace(0.0, 1.0, half, dtype=F32))
    ang = pos[:, None] * theta[None, :]
    cos, sin = jnp.cos(ang), jnp.sin(ang)
    cos_t = jnp.tile(cos, (1, LANES // half))
    sin_t = jnp.tile(jnp.concatenate([-sin, sin], axis=-1), (1, LANES // HEAD_DIM))
    return cos_t, sin_t


def _retention_tables():
    L = RET_CHUNK
    h = B_HEADS
    log_gamma = jnp.log(1.0 - jnp.exp2(-5.0 - jnp.arange(h, dtype=F32)))
    pos = jnp.arange(L, dtype=F32)
    diff = pos[:, None] - pos[None, :]
    decay = jnp.where(diff >= 0, jnp.exp(log_gamma[:, None, None] * jnp.maximum(diff, 0.0)), 0.0)
    zeta = jnp.exp(log_gamma[:, None] * (L - 1.0 - pos))
    xi = jnp.exp(log_gamma[:, None] * (pos + 1.0))
    chunk_decay = jnp.exp(log_gamma * L)
    per_lane = lambda t: jnp.repeat(t.T, HEAD_DIM, axis=1)
    cd_lane = jnp.repeat(chunk_decay, HEAD_DIM).reshape(B_WIDTH // LANES, LANES, 1)
    cd = jnp.broadcast_to(cd_lane, (B_WIDTH // LANES, LANES, LANES))
    return decay, per_lane(zeta), per_lane(xi), cd


def _alibi_slopes():
    return jnp.exp2(-8.0 * jnp.arange(1, A_HEADS + 1, dtype=F32) / A_HEADS) * LOG2E


def kernel(x, norm_w, w_in, conv_w, conv_b, gate_bias, ret_norm_w, mlstm_norm_w, w_out, final_norm_w):
    b, s, d = x.shape
    assert b == 1 and d == D_MODEL and s % PROJ_ROWS == 0 and s % OUT_ROWS == 0
    depth = w_in.shape[0]
    cos_t, sin_t = _rope_tables(s)
    decay, zeta, xi, cd = _retention_tables()
    slopes = _alibi_slopes()

    w_main = w_in[:, :, :N_MAIN].astype(BF16)
    w_gate = jnp.pad(w_in[:, :, N_MAIN:], ((0, 0), (0, 0), (0, LANES - 2 * C_HEADS)))
    w_gate = w_gate.astype(BF16)
    w_out_b = w_out.astype(BF16)
    gb = jnp.pad(gate_bias.astype(F32), ((0, 0), (0, LANES - 2 * C_HEADS)))

    xs = x[0]
    for layer in range(depth):
        qT, ka, vT, ga, kmean, ret_in, mls_in, gates = _proj(
            xs, norm_w[layer][None, :], w_main[layer], w_gate[layer], cos_t, sin_t,
            conv_w[layer], conv_b[layer][None, :])
        ya = _moba(slopes, qT, ka, vT, kmean.reshape(-1, A_WIDTH), ga)
        yb = _ret(ret_in, decay, zeta, xi, cd, ret_norm_w[layer][None, :])
        yc = _mlstm(mls_in, gates, gb[layer][None, :], mlstm_norm_w[layer][None, :])
        wo = w_out_b[layer]
        xs = _outproj(xs, ya, yb, yc, wo[:A_WIDTH], wo[A_WIDTH:A_WIDTH + B_WIDTH], wo[A_WIDTH + B_WIDTH:],
                      final_norm_w[None, :], final_norm=(layer == depth - 1))
    return xs[None]
```

```python
import functools

import numpy as np
import jax
import jax.numpy as jnp
from jax import lax
from jax.experimental import pallas as pl
from jax.experimental.pallas import tpu as pltpu

D_MODEL = 1024
HEAD_DIM = 64
A_HEADS, B_HEADS, C_HEADS = 8, 4, 4
A_WIDTH, B_WIDTH, C_WIDTH = A_HEADS * HEAD_DIM, B_HEADS * HEAD_DIM, C_HEADS * HEAD_DIM
MOBA_BLOCK = 256
MOBA_TOPK = 3
MOBA_GROUP = 4
MOBA_BIG_GROUP = 16
RET_CHUNK = 256
MLSTM_CHUNK = 128
MLSTM_STEP_CHUNKS = 4
CONV_K = 4
RMS_EPS = 1e-6
HEAD_NORM_EPS = 1e-5
ROPE_BASE = 10000.0
N_MAIN = 4 * A_WIDTH + 4 * B_WIDTH + 5 * C_WIDTH
QK_SCALE = HEAD_DIM ** -0.5
V_ROWS = HEAD_DIM + 16
K_LANES = 2 * 128
ALIBI_PARTS = 3
LOG2E = 1.4426950408889634

LANES = 128
CONV_PAD = 8
PROJ_ROWS = 512
OUT_ROWS = 512
VMEM_LIMIT = 56 * 1024 * 1024

BF16 = jnp.bfloat16
F32 = jnp.float32
NEG_INF = float("-inf")


def _silu(t):
    return t * jax.nn.sigmoid(t)


def _swap_half_heads(t):
    lane = lax.broadcasted_iota(jnp.int32, t.shape, 1)
    first_half = (lane % HEAD_DIM) < (HEAD_DIM // 2)
    return jnp.where(first_half, pltpu.roll(t, LANES - HEAD_DIM // 2, 1), pltpu.roll(t, HEAD_DIM // 2, 1))


def _proj_kernel(x_ref, nw_ref, w_ref, wg_ref, cos_ref, sin_ref, cw_ref, cb_ref,
                 qT_ref, ka_ref, vT_ref, ga_ref, kmean_ref, ret_ref, mls_ref, gates_ref,
                 h_scr, ext_scr):
    i = pl.program_id(0)
    tm = x_ref.shape[0]
    nblk = tm // MOBA_BLOCK

    x = x_ref[...]
    ms = jnp.mean(x * x, axis=-1, keepdims=True)
    h_scr[...] = ((x * lax.rsqrt(ms + RMS_EPS)) * nw_ref[...]).astype(BF16)

    def seg(lo, width):
        return jnp.dot(h_scr[...], w_ref[:, lo:lo + width], preferred_element_type=F32)

    aq = seg(0, A_WIDTH) * (QK_SCALE * LOG2E)
    ak = seg(A_WIDTH, A_WIDTH)
    av = seg(2 * A_WIDTH, A_WIDTH)
    row_in_blk = lax.broadcasted_iota(jnp.int32, (tm, LANES), 0) % MOBA_BLOCK
    feat_lane = lax.broadcasted_iota(jnp.int32, (tm, LANES), 1)
    feat = jnp.where(feat_lane < ALIBI_PARTS, row_in_blk, 0).astype(F32).astype(BF16)
    for pr in range(A_HEADS // 2):
        ka_ref[:, pr * K_LANES:pr * K_LANES + LANES] = ak[:, pr * LANES:(pr + 1) * LANES].astype(BF16)
        ka_ref[:, pr * K_LANES + LANES:(pr + 1) * K_LANES] = feat
    ga_ref[...] = seg(3 * A_WIDTH, A_WIDTH).astype(BF16)
    for r in range(nblk):
        rows = slice(r * MOBA_BLOCK, (r + 1) * MOBA_BLOCK)
        qT_ref[r] = aq[rows].T.astype(BF16)
        avT = av[rows].T
        for h in range(A_HEADS):
            vT_ref[r, h * V_ROWS:h * V_ROWS + HEAD_DIM, :] = avT[h * HEAD_DIM:(h + 1) * HEAD_DIM].astype(BF16)
            vT_ref[r, h * V_ROWS + HEAD_DIM:(h + 1) * V_ROWS, :] = jnp.ones((V_ROWS - HEAD_DIM, MOBA_BLOCK), BF16)
        kmean_ref[r] = jnp.mean(ak[rows], axis=0, keepdims=True)

    b0 = 4 * A_WIDTH
    cos = cos_ref[...]
    sin = sin_ref[...]
    for t in range(2 * B_WIDTH // LANES):
        v = seg(b0 + t * LANES, LANES)
        rot = v * cos + _swap_half_heads(v) * sin
        if t >= B_WIDTH // LANES:
            rot = rot * QK_SCALE
        ret_ref[:, t * LANES:(t + 1) * LANES] = rot.astype(BF16)
    ret_ref[:, 2 * B_WIDTH:4 * B_WIDTH] = seg(b0 + 2 * B_WIDTH, 2 * B_WIDTH).astype(BF16)

    c0 = b0 + 4 * B_WIDTH

    @pl.when(i == 0)
    def _():
        ext_scr[0:CONV_PAD, :] = jnp.zeros((CONV_PAD, 2 * C_WIDTH), F32)

    ext_scr[CONV_PAD:CONV_PAD + tm, :] = seg(c0, 2 * C_WIDTH)
    y = jnp.zeros((tm, 2 * C_WIDTH), F32) + cb_ref[...]
    for tap in range(CONV_K):
        lo = CONV_PAD - (CONV_K - 1) + tap
        y = y + cw_ref[tap:tap + 1, :] * ext_scr[lo:lo + tm, :]
    ext_scr[0:CONV_PAD, :] = ext_scr[tm:tm + CONV_PAD, :]
    qk = _silu(y)
    mls_ref[:, 0:C_WIDTH] = qk[:, 0:C_WIDTH].astype(BF16)
    mls_ref[:, C_WIDTH:2 * C_WIDTH] = (qk[:, C_WIDTH:] * QK_SCALE).astype(BF16)
    mls_ref[:, 2 * C_WIDTH:5 * C_WIDTH] = seg(c0 + 2 * C_WIDTH, 3 * C_WIDTH).astype(BF16)

    gates_ref[...] = jnp.dot(h_scr[...], wg_ref[...], preferred_element_type=F32)


def _proj(x, norm_w, w_main, w_gate, cos_t, sin_t, conv_w, conv_b):
    s = x.shape[0]
    tm = PROJ_ROWS
    nblk = tm // MOBA_BLOCK
    nb = s // MOBA_BLOCK
    full = lambda shape: pl.BlockSpec(shape, lambda i: (0,) * len(shape))
    return pl.pallas_call(
        _proj_kernel,
        grid=(s // tm,),
        in_specs=[
            pl.BlockSpec((tm, D_MODEL), lambda i: (i, 0)),
            full((1, D_MODEL)),
            full((D_MODEL, N_MAIN)),
            full((D_MODEL, LANES)),
            pl.BlockSpec((tm, LANES), lambda i: (i, 0)),
            pl.BlockSpec((tm, LANES), lambda i: (i, 0)),
            full((CONV_K, 2 * C_WIDTH)),
            full((1, 2 * C_WIDTH)),
        ],
        out_specs=[
            pl.BlockSpec((nblk, A_WIDTH, MOBA_BLOCK), lambda i: (i, 0, 0)),
            pl.BlockSpec((tm, (A_HEADS // 2) * K_LANES), lambda i: (i, 0)),
            pl.BlockSpec((nblk, A_HEADS * V_ROWS, MOBA_BLOCK), lambda i: (i, 0, 0)),
            pl.BlockSpec((tm, A_WIDTH), lambda i: (i, 0)),
            pl.BlockSpec((nblk, 1, A_WIDTH), lambda i: (i, 0, 0)),
            pl.BlockSpec((tm, 4 * B_WIDTH), lambda i: (i, 0)),
            pl.BlockSpec((tm, 5 * C_WIDTH), lambda i: (i, 0)),
            pl.BlockSpec((tm, LANES), lambda i: (i, 0)),
        ],
        out_shape=[
            jax.ShapeDtypeStruct((nb, A_WIDTH, MOBA_BLOCK), BF16),
            jax.ShapeDtypeStruct((s, (A_HEADS // 2) * K_LANES), BF16),
            jax.ShapeDtypeStruct((nb, A_HEADS * V_ROWS, MOBA_BLOCK), BF16),
            jax.ShapeDtypeStruct((s, A_WIDTH), BF16),
            jax.ShapeDtypeStruct((nb, 1, A_WIDTH), F32),
            jax.ShapeDtypeStruct((s, 4 * B_WIDTH), BF16),
            jax.ShapeDtypeStruct((s, 5 * C_WIDTH), BF16),
            jax.ShapeDtypeStruct((s, LANES), F32),
        ],
        scratch_shapes=[
            pltpu.VMEM((tm, D_MODEL), BF16),
            pltpu.VMEM((tm + CONV_PAD, 2 * C_WIDTH), F32),
        ],
        compiler_params=pltpu.CompilerParams(
            dimension_semantics=("arbitrary",), vmem_limit_bytes=VMEM_LIMIT),
        name="proj",
    )(x, norm_w, w_main, w_gate, cos_t, sin_t, conv_w, conv_b)


def _moba_kernel(slopes_ref, alibi_ref, qT_ref, k_ref, vT_ref, kmean_ref, g_ref, o_ref,
                 qm_scr, sel_scr, m_scr, acc_scr, s_scr, gate_scr, own_scr):
    p = pl.program_id(0)
    j = pl.program_id(1)
    blk = MOBA_BLOCK
    nb = kmean_ref.shape[0]

    qT2 = qT_ref[0]
    dim_row = lax.broadcasted_iota(jnp.int32, (LANES, blk), 0)
    blk_row = lax.broadcasted_iota(jnp.int32, (nb, blk), 0)
    key_off = lax.broadcasted_iota(jnp.int32, (blk, blk), 0)
    qry_off = lax.broadcasted_iota(jnp.int32, (blk, blk), 1)
    kmean = kmean_ref[...].astype(BF16)
    k_own = k_ref[pl.ds(pl.multiple_of(j * blk, blk), blk), :]
    vT_own = vT_ref[j]

    for hh in range(2):
        qm = jnp.where((dim_row // HEAD_DIM) == hh, qT2, jnp.zeros_like(qT2))
        coef = jnp.zeros((LANES, blk), F32)
        for part in range(ALIBI_PARTS):
            coef = jnp.where(dim_row == part, alibi_ref[2 * p + hh, part], coef)
        qa = jnp.concatenate([qm, coef.astype(BF16)], axis=0)
        qm_scr[hh] = qa
        gate_scr[hh] = jnp.dot(kmean, qm, preferred_element_type=F32)
        own_scr[hh] = jnp.dot(k_own, qa, preferred_element_type=F32)

    for hh in range(2):
        work = jnp.where(blk_row < j, gate_scr[hh], NEG_INF)
        sel = jnp.zeros((nb, blk), F32)
        for r in range(MOBA_TOPK):
            mx = jnp.max(work, axis=0, keepdims=True)
            idx = jnp.min(jnp.where(work == mx, blk_row, nb), axis=0, keepdims=True)
            pick = blk_row == idx
            sel = jnp.where(jnp.logical_and(pick, r < j), 1.0, sel)
            work = jnp.where(pick, NEG_INF, work)
        sel_scr[hh] = sel

        s = jnp.where(key_off <= qry_off, own_scr[hh], NEG_INF)
        m = jnp.max(s, axis=0, keepdims=True)
        pm = jnp.exp2(s - m)
        m_scr[hh] = m
        acc_scr[hh] = jnp.dot(vT_own[hh * V_ROWS:(hh + 1) * V_ROWS, :], pm.astype(BF16),
                              preferred_element_type=F32)

    def make_group(size, first_block):
        def group(g, carry):
            base = first_block + g * size
            for b in range(size):
                k_blk = k_ref[pl.ds(pl.multiple_of((base + b) * blk, blk), blk), :]
                for hh in range(2):
                    s_scr[2 * b + hh] = jnp.dot(k_blk, qm_scr[hh], preferred_element_type=F32)
            parts = [[], []]
            for b in range(size):
                n = base + b
                vT_blk = vT_ref[n]
                dist = ((n - j) * blk).astype(F32)
                for hh in range(2):
                    s = s_scr[2 * b + hh]
                    m_loc = jnp.max(s, axis=0, keepdims=True)
                    pm = jnp.exp2(s - m_loc)
                    o_loc = jnp.dot(vT_blk[hh * V_ROWS:(hh + 1) * V_ROWS, :], pm.astype(BF16),
                                    preferred_element_type=F32)
                    chosen = sel_scr[hh, pl.ds(n, 1), :] > 0.0
                    cand = jnp.where(chosen, m_loc + slopes_ref[2 * p + hh] * dist, NEG_INF)
                    parts[hh].append((cand, o_loc))
            for hh in range(2):
                m_old = m_scr[hh]
                m_new = m_old
                for cand, _ in parts[hh]:
                    m_new = jnp.maximum(m_new, cand)
                acc = jnp.exp2(m_old - m_new) * acc_scr[hh]
                for cand, o_loc in parts[hh]:
                    acc = acc + jnp.exp2(cand - m_new) * o_loc
                m_scr[hh] = m_new
                acc_scr[hh] = acc
            return carry
        return group

    n_big = j // MOBA_BIG_GROUP
    done = n_big * MOBA_BIG_GROUP
    lax.fori_loop(0, n_big, make_group(MOBA_BIG_GROUP, 0), 0)
    lax.fori_loop(0, (j - done + MOBA_GROUP - 1) // MOBA_GROUP, make_group(MOBA_GROUP, done), 0)

    oT = jnp.concatenate([acc_scr[hh, 0:HEAD_DIM, :] / acc_scr[hh, HEAD_DIM:HEAD_DIM + 1, :] for hh in range(2)],
                         axis=0)
    o_ref[...] = (oT.T * _silu(g_ref[...].astype(F32))).astype(o_ref.dtype)


def _moba(slopes, alibi_parts, qT, ka, vT, kmean, ga):
    s = ka.shape[0]
    nb = s // MOBA_BLOCK
    blk = MOBA_BLOCK
    npair = A_HEADS // 2
    return pl.pallas_call(
        _moba_kernel,
        grid=(npair, nb),
        in_specs=[
            pl.BlockSpec(memory_space=pltpu.SMEM),
            pl.BlockSpec(memory_space=pltpu.SMEM),
            pl.BlockSpec((1, LANES, blk), lambda p, j: (j, p, 0)),
            pl.BlockSpec((s, K_LANES), lambda p, j: (0, p)),
            pl.BlockSpec((nb, 2 * V_ROWS, blk), lambda p, j: (0, p, 0)),
            pl.BlockSpec((nb, LANES), lambda p, j: (0, p)),
            pl.BlockSpec((blk, LANES), lambda p, j: (j, p)),
        ],
        out_specs=pl.BlockSpec((blk, LANES), lambda p, j: (j, p)),
        out_shape=jax.ShapeDtypeStruct((s, A_WIDTH), BF16),
        scratch_shapes=[
            pltpu.VMEM((2, K_LANES, blk), BF16),
            pltpu.VMEM((2, nb, blk), F32),
            pltpu.VMEM((2, 1, blk), F32),
            pltpu.VMEM((2, V_ROWS, blk), F32),
            pltpu.VMEM((2 * MOBA_BIG_GROUP, blk, blk), F32),
            pltpu.VMEM((2, nb, blk), F32),
            pltpu.VMEM((2, blk, blk), F32),
        ],
        compiler_params=pltpu.CompilerParams(
            dimension_semantics=("arbitrary", "arbitrary"), vmem_limit_bytes=VMEM_LIMIT),
        name="moba",
    )(slopes, alibi_parts, qT, ka, vT, kmean, ga)


def _head_norm_pair(y, lane_head):
    mu = jnp.zeros_like(y)
    for hh in range(2):
        in_h = lane_head == hh
        mu_h = jnp.sum(jnp.where(in_h, y, 0.0), axis=-1, keepdims=True) * (1.0 / HEAD_DIM)
        mu = jnp.where(in_h, mu_h, mu)
    d = y - mu
    var = jnp.zeros_like(y)
    for hh in range(2):
        in_h = lane_head == hh
        var_h = jnp.sum(jnp.where(in_h, d * d, 0.0), axis=-1, keepdims=True) * (1.0 / HEAD_DIM)
        var = jnp.where(in_h, var_h, var)
    return d * lax.rsqrt(var + HEAD_NORM_EPS)


def _ret_kernel(x_ref, decay_ref, zeta_ref, xi_ref, cd_ref, nw_ref, o_ref, state_scr):
    c = pl.program_id(0)
    L = RET_CHUNK

    @pl.when(c == 0)
    def _():
        state_scr[...] = jnp.zeros_like(state_scr)

    lane_head = lax.broadcasted_iota(jnp.int32, (L, LANES), 1) // HEAD_DIM
    sq_row = lax.broadcasted_iota(jnp.int32, (LANES, LANES), 0) // HEAD_DIM
    sq_col = lax.broadcasted_iota(jnp.int32, (LANES, LANES), 1) // HEAD_DIM
    for t in range(B_WIDTH // LANES):
        cols = slice(t * LANES, (t + 1) * LANES)
        q2 = x_ref[:, t * LANES:(t + 1) * LANES]
        k2 = x_ref[:, B_WIDTH + t * LANES:B_WIDTH + (t + 1) * LANES]
        v2 = x_ref[:, 2 * B_WIDTH + t * LANES:2 * B_WIDTH + (t + 1) * LANES]
        g2 = x_ref[:, 3 * B_WIDTH + t * LANES:3 * B_WIDTH + (t + 1) * LANES]
        state = state_scr[t]
        y = jnp.dot(q2, state.astype(BF16), preferred_element_type=F32) * xi_ref[:, cols]
        for hh in range(2):
            qm = jnp.where(lane_head == hh, q2, jnp.zeros_like(q2))
            sc = lax.dot_general(qm, k2, (((1,), (1,)), ((), ())), preferred_element_type=F32)
            sc = sc * decay_ref[2 * t + hh]
            o_h = jnp.dot(sc.astype(BF16), v2, preferred_element_type=F32)
            y = y + jnp.where(lane_head == hh, o_h, 0.0)
        kz = (k2.astype(F32) * zeta_ref[:, cols]).astype(BF16)
        kv = lax.dot_general(kz, v2, (((0,), (0,)), ((), ())), preferred_element_type=F32)
        state_scr[t] = cd_ref[t] * state + jnp.where(sq_row == sq_col, kv, 0.0)
        yn = _head_norm_pair(y, lane_head) * nw_ref[:, cols]
        o_ref[:, cols] = (yn * _silu(g2.astype(F32))).astype(o_ref.dtype)


def _ret(ret_in, decay, zeta, xi, cd, norm_w):
    s = ret_in.shape[0]
    L = RET_CHUNK
    full = lambda shape: pl.BlockSpec(shape, lambda c: (0,) * len(shape))
    return pl.pallas_call(
        _ret_kernel,
        grid=(s // L,),
        in_specs=[
            pl.BlockSpec((L, 4 * B_WIDTH), lambda c: (c, 0)),
            full((B_HEADS, L, L)),
            full((L, B_WIDTH)),
            full((L, B_WIDTH)),
            full((B_WIDTH // LANES, LANES, LANES)),
            full((1, B_WIDTH)),
        ],
        out_specs=pl.BlockSpec((L, B_WIDTH), lambda c: (c, 0)),
        out_shape=jax.ShapeDtypeStruct((s, B_WIDTH), BF16),
        scratch_shapes=[pltpu.VMEM((B_WIDTH // LANES, LANES, LANES), F32)],
        compiler_params=pltpu.CompilerParams(
            dimension_semantics=("arbitrary",), vmem_limit_bytes=VMEM_LIMIT),
        name="ret",
    )(ret_in, decay, zeta, xi, cd, norm_w)


def _mlstm_kernel(x_ref, gates_ref, gb_ref, nw_ref, o_ref, st_scr, m_scr, qk_scr):
    step = pl.program_id(0)
    L = MLSTM_CHUNK
    ntile = C_WIDTH // LANES

    @pl.when(step == 0)
    def _():
        st_scr[...] = jnp.zeros_like(st_scr)
        m_scr[...] = jnp.zeros_like(m_scr)

    lane_head = lax.broadcasted_iota(jnp.int32, (L, LANES), 1) // HEAD_DIM
    lane_head1 = lane_head[0:1, :]
    sq_row = lax.broadcasted_iota(jnp.int32, (LANES, 2 * LANES), 0) // HEAD_DIM
    sq_col = (lax.broadcasted_iota(jnp.int32, (LANES, 2 * LANES), 1) % LANES) // HEAD_DIM
    same_head = sq_row == sq_col
    t_row = lax.broadcasted_iota(jnp.int32, (L, L), 0)
    t_col = lax.broadcasted_iota(jnp.int32, (L, L), 1)
    causal = t_row >= t_col
    lower = causal.astype(F32)
    ones_tile = jnp.ones((L, LANES), BF16)

    for c in range(MLSTM_STEP_CHUNKS):
        rows = slice(c * L, (c + 1) * L)
        for t in range(ntile):
            q2 = x_ref[rows, t * LANES:(t + 1) * LANES]
            k2 = x_ref[rows, C_WIDTH + t * LANES:C_WIDTH + (t + 1) * LANES]
            for hh in range(2):
                qm = jnp.where(lane_head == hh, q2, jnp.zeros_like(q2))
                qk_scr[(c * ntile + t) * 2 + hh] = lax.dot_general(
                    qm, k2, (((1,), (1,)), ((), ())), preferred_element_type=F32)

    local = []
    for c in range(MLSTM_STEP_CHUNKS):
        rows = slice(c * L, (c + 1) * L)
        pre = gates_ref[rows, :] + gb_ref[...]
        logsig = jnp.minimum(pre, 0.0) - jnp.log1p(jnp.exp(-jnp.abs(pre)))
        bsum = jnp.dot(lower, logsig, preferred_element_type=F32, precision=lax.Precision.HIGHEST)
        pre_t = pre.T
        bsum_t = bsum.T
        for t in range(ntile):
            k2 = x_ref[rows, C_WIDTH + t * LANES:C_WIDTH + (t + 1) * LANES]
            v2 = x_ref[rows, 2 * C_WIDTH + t * LANES:2 * C_WIDTH + (t + 1) * LANES]
            v_aug = jnp.concatenate([v2, ones_tile], axis=1)
            num_l = jnp.zeros((L, LANES), F32)
            den_l = jnp.zeros((L, LANES), F32)
            mloc_l = jnp.zeros((L, LANES), F32)
            b_l = jnp.zeros((L, LANES), F32)
            wg_l = jnp.zeros((L, LANES), F32)
            gmax_l = jnp.zeros((1, LANES), F32)
            blast_l = jnp.zeros((1, LANES), F32)
            for hh in range(2):
                h = 2 * t + hh
                in_h = lane_head == hh
                in_h1 = lane_head1 == hh
                i_col = pre[:, h:h + 1]
                b_col = bsum[:, C_HEADS + h:C_HEADS + h + 1]
                i_row = pre_t[h:h + 1, :]
                b_row = bsum_t[C_HEADS + h:C_HEADS + h + 1, :]
                d_log = jnp.where(causal, b_col - b_row + i_row, NEG_INF)
                m_loc = jnp.max(d_log, axis=-1, keepdims=True)
                sc = qk_scr[(c * ntile + t) * 2 + hh] * jnp.exp(d_log - m_loc)
                nd = jnp.dot(sc.astype(BF16), v_aug, preferred_element_type=F32)
                num_l = jnp.where(in_h, nd[:, 0:LANES], num_l)
                den_l = jnp.where(in_h, nd[:, LANES:2 * LANES], den_l)
                mloc_l = jnp.where(in_h, m_loc, mloc_l)
                b_l = jnp.where(in_h, b_col, b_l)
                b_last = b_col[L - 1:L, :]
                g_col = b_last - b_col + i_col
                g_max = jnp.max(g_col, axis=0, keepdims=True)
                wg_l = jnp.where(in_h, jnp.exp(g_col - g_max), wg_l)
                gmax_l = jnp.where(in_h1, g_max, gmax_l)
                blast_l = jnp.where(in_h1, b_last, blast_l)
            kw = (k2.astype(F32) * wg_l).astype(BF16)
            kv = lax.dot_general(kw, v_aug, (((0,), (0,)), ((), ())), preferred_element_type=F32)
            kv = jnp.where(same_head, kv, 0.0)
            local.append((num_l, den_l, mloc_l, b_l, gmax_l, blast_l, kv))

    for t in range(ntile):
        cols = slice(t * LANES, (t + 1) * LANES)
        state = st_scr[t]
        m_st = m_scr[t]
        for c in range(MLSTM_STEP_CHUNKS):
            rows = slice(c * L, (c + 1) * L)
            num_l, den_l, mloc_l, b_l, gmax_l, blast_l, kv = local[c * ntile + t]
            q2 = x_ref[rows, t * LANES:(t + 1) * LANES]
            o2 = x_ref[rows, 3 * C_WIDTH + t * LANES:3 * C_WIDTH + (t + 1) * LANES]
            g2 = x_ref[rows, 4 * C_WIDTH + t * LANES:4 * C_WIDTH + (t + 1) * LANES]
            qs = jnp.dot(q2, state.astype(BF16), preferred_element_type=F32)
            inter_log = b_l + m_st
            m_t = jnp.maximum(inter_log, mloc_l)
            r_intra = jnp.exp(mloc_l - m_t)
            w_inter = jnp.exp(inter_log - m_t)
            num = r_intra * num_l + w_inter * qs[:, 0:LANES]
            den = r_intra * den_l + w_inter * qs[:, LANES:2 * LANES]
            h_t = num / jnp.maximum(jnp.abs(den), jnp.exp(-m_t))
            m_new = jnp.maximum(blast_l + m_st, gmax_l)
            a = jnp.exp(blast_l + m_st - m_new)
            beta = jnp.exp(gmax_l - m_new)
            a2 = jnp.concatenate([a, a], axis=1)
            beta2 = jnp.concatenate([beta, beta], axis=1)
            state = a2 * state + beta2 * kv
            m_st = m_new
            y = h_t * jax.nn.sigmoid(o2.astype(F32))
            yn = _head_norm_pair(y, lane_head) * nw_ref[:, cols]
            o_ref[rows, cols] = (yn * _silu(g2.astype(F32))).astype(o_ref.dtype)
        st_scr[t] = state
        m_scr[t] = m_st


def _mlstm(mls_in, gates, gate_bias, norm_w):
    s = mls_in.shape[0]
    rows = MLSTM_CHUNK * MLSTM_STEP_CHUNKS
    npair = C_WIDTH // LANES
    full = lambda shape: pl.BlockSpec(shape, lambda c: (0,) * len(shape))
    return pl.pallas_call(
        _mlstm_kernel,
        grid=(s // rows,),
        in_specs=[
            pl.BlockSpec((rows, 5 * C_WIDTH), lambda c: (c, 0)),
            pl.BlockSpec((rows, LANES), lambda c: (c, 0)),
            full((1, LANES)),
            full((1, C_WIDTH)),
        ],
        out_specs=pl.BlockSpec((rows, C_WIDTH), lambda c: (c, 0)),
        out_shape=jax.ShapeDtypeStruct((s, C_WIDTH), BF16),
        scratch_shapes=[
            pltpu.VMEM((npair, LANES, 2 * LANES), F32),
            pltpu.VMEM((npair, 1, LANES), F32),
            pltpu.VMEM((MLSTM_STEP_CHUNKS * npair * 2, MLSTM_CHUNK, MLSTM_CHUNK), F32),
        ],
        compiler_params=pltpu.CompilerParams(
            dimension_semantics=("arbitrary",), vmem_limit_bytes=VMEM_LIMIT),
        name="mlstm",
    )(mls_in, gates, gate_bias, norm_w)


def _outproj_kernel(x_ref, ya_ref, yb_ref, yc_ref, wa_ref, wb_ref, wc_ref, fw_ref, o_ref, *, final_norm):
    acc = x_ref[...]
    acc = acc + jnp.dot(ya_ref[...], wa_ref[...], preferred_element_type=F32)
    acc = acc + jnp.dot(yb_ref[...], wb_ref[...], preferred_element_type=F32)
    acc = acc + jnp.dot(yc_ref[...], wc_ref[...], preferred_element_type=F32)
    if final_norm:
        ms = jnp.mean(acc * acc, axis=-1, keepdims=True)
        acc = (acc * lax.rsqrt(ms + RMS_EPS)) * fw_ref[...]
    o_ref[...] = acc


def _outproj(x, ya, yb, yc, wa, wb, wc, final_w, final_norm):
    s = x.shape[0]
    tm = OUT_ROWS
    full = lambda shape: pl.BlockSpec(shape, lambda i: (0,) * len(shape))
    rows = lambda width: pl.BlockSpec((tm, width), lambda i: (i, 0))
    return pl.pallas_call(
        functools.partial(_outproj_kernel, final_norm=final_norm),
        grid=(s // tm,),
        in_specs=[rows(D_MODEL), rows(A_WIDTH), rows(B_WIDTH), rows(C_WIDTH),
                  full((A_WIDTH, D_MODEL)), full((B_WIDTH, D_MODEL)), full((C_WIDTH, D_MODEL)),
                  full((1, D_MODEL))],
        out_specs=rows(D_MODEL),
        out_shape=jax.ShapeDtypeStruct((s, D_MODEL), F32),
        compiler_params=pltpu.CompilerParams(
            dimension_semantics=("arbitrary",), vmem_limit_bytes=VMEM_LIMIT),
        name="outproj",
    )(x, ya, yb, yc, wa, wb, wc, final_w)


def _rope_tables(s):
    half = HEAD_DIM // 2
    pos = jnp.arange(s, dtype=F32)
    theta = 1.0 / (ROPE_BASE ** jnp.linspace(0.0, 1.0, half, dtype=F32))
    ang = pos[:, None] * theta[None, :]
    cos, sin = jnp.cos(ang), jnp.sin(ang)
    cos_t = jnp.tile(cos, (1, LANES // half))
    sin_t = jnp.tile(jnp.concatenate([-sin, sin], axis=-1), (1, LANES // HEAD_DIM))
    return cos_t, sin_t


def _retention_tables():
    L = RET_CHUNK
    h = B_HEADS
    log_gamma = jnp.log(1.0 - jnp.exp2(-5.0 - jnp.arange(h, dtype=F32)))
    pos = jnp.arange(L, dtype=F32)
    diff = pos[:, None] - pos[None, :]
    decay = jnp.where(diff >= 0, jnp.exp(log_gamma[:, None, None] * jnp.maximum(diff, 0.0)), 0.0)
    zeta = jnp.exp(log_gamma[:, None] * (L - 1.0 - pos))
    xi = jnp.exp(log_gamma[:, None] * (pos + 1.0))
    chunk_decay = jnp.exp(log_gamma * L)
    per_lane = lambda t: jnp.repeat(t.T, HEAD_DIM, axis=1)
    cd_lane = jnp.repeat(chunk_decay, HEAD_DIM).reshape(B_WIDTH // LANES, LANES, 1)
    cd = jnp.broadcast_to(cd_lane, (B_WIDTH // LANES, LANES, LANES))
    return decay, per_lane(zeta), per_lane(xi), cd


def _alibi_slopes():
    return jnp.exp2(-8.0 * jnp.arange(1, A_HEADS + 1, dtype=F32) / A_HEADS) * LOG2E


def _split_bf16(t):
    parts, rest = [], t
    for _ in range(ALIBI_PARTS):
        piece = rest.astype(BF16).astype(F32)
        parts.append(piece)
        rest = rest - piece
    return jnp.stack(parts, axis=-1)


def kernel(x, norm_w, w_in, conv_w, conv_b, gate_bias, ret_norm_w, mlstm_norm_w, w_out, final_norm_w):
    b, s, d = x.shape
    assert b == 1 and d == D_MODEL and s % PROJ_ROWS == 0 and s % OUT_ROWS == 0
    assert s % (MLSTM_CHUNK * MLSTM_STEP_CHUNKS) == 0
    depth = w_in.shape[0]
    cos_t, sin_t = _rope_tables(s)
    decay, zeta, xi, cd = _retention_tables()
    slopes = _alibi_slopes()
    alibi_parts = _split_bf16(slopes)

    w_main = w_in[:, :, :N_MAIN].astype(BF16)
    w_gate = jnp.pad(w_in[:, :, N_MAIN:], ((0, 0), (0, 0), (0, LANES - 2 * C_HEADS)))
    w_gate = w_gate.astype(BF16)
    w_out_b = w_out.astype(BF16)
    gb = jnp.pad(gate_bias.astype(F32), ((0, 0), (0, LANES - 2 * C_HEADS)))

    xs = x[0]
    for layer in range(depth):
        qT, ka, vT, ga, kmean, ret_in, mls_in, gates = _proj(
            xs, norm_w[layer][None, :], w_main[layer], w_gate[layer], cos_t, sin_t,
            conv_w[layer], conv_b[layer][None, :])
        ya = _moba(slopes, alibi_parts, qT, ka, vT, kmean.reshape(-1, A_WIDTH), ga)
        yb = _ret(ret_in, decay, zeta, xi, cd, ret_norm_w[layer][None, :])
        yc = _mlstm(mls_in, gates, gb[layer][None, :], mlstm_norm_w[layer][None, :])
        wo = w_out_b[layer]
        xs = _outproj(xs, ya, yb, yc, wo[:A_WIDTH], wo[A_WIDTH:A_WIDTH + B_WIDTH], wo[A_WIDTH + B_WIDTH:],
                      final_norm_w[None, :], final_norm=(layer == depth - 1))
    return xs[None]
```

```python
import functools

import numpy as np
import jax
import jax.numpy as jnp
from jax import lax
from jax.experimental import pallas as pl
from jax.experimental.pallas import tpu as pltpu

D_MODEL = 1024
HEAD_DIM = 64
A_HEADS, B_HEADS, C_HEADS = 8, 4, 4
A_WIDTH, B_WIDTH, C_WIDTH = A_HEADS * HEAD_DIM, B_HEADS * HEAD_DIM, C_HEADS * HEAD_DIM
MOBA_BLOCK = 256
MOBA_TOPK = 3
MOBA_GROUP = 4
MOBA_BIG_GROUP = 16
MOBA_HUGE_GROUP = 32
RET_CHUNK = 256
RET_STEP_CHUNKS = 4
MLSTM_CHUNK = 128
MLSTM_STEP_CHUNKS = 4
CONV_K = 4
RMS_EPS = 1e-6
HEAD_NORM_EPS = 1e-5
ROPE_BASE = 10000.0
N_MAIN = 4 * A_WIDTH + 4 * B_WIDTH + 5 * C_WIDTH
QK_SCALE = HEAD_DIM ** -0.5
V_ROWS = HEAD_DIM + 16
K_LANES = 2 * 128
ALIBI_PARTS = 3
LOG2E = 1.4426950408889634

LANES = 128
CONV_PAD = 8
PROJ_ROWS = 512
OUT_ROWS = 512
VMEM_LIMIT = 56 * 1024 * 1024

BF16 = jnp.bfloat16
F32 = jnp.float32
NEG_INF = float("-inf")


def _silu(t):
    return t * jax.nn.sigmoid(t)


def _swap_half_heads(t):
    lane = lax.broadcasted_iota(jnp.int32, t.shape, 1)
    first_half = (lane % HEAD_DIM) < (HEAD_DIM // 2)
    return jnp.where(first_half, pltpu.roll(t, LANES - HEAD_DIM // 2, 1), pltpu.roll(t, HEAD_DIM // 2, 1))


def _proj_kernel(x_ref, nw_ref, w_ref, wg_ref, cos_ref, sin_ref, cw_ref, cb_ref,
                 qT_ref, ka_ref, vT_ref, ga_ref, kmean_ref, ret_ref, mls_ref, gates_ref,
                 h_scr, ext_scr):
    i = pl.program_id(0)
    tm = x_ref.shape[0]
    nblk = tm // MOBA_BLOCK

    x = x_ref[...]
    ms = jnp.mean(x * x, axis=-1, keepdims=True)
    h_scr[...] = ((x * lax.rsqrt(ms + RMS_EPS)) * nw_ref[...]).astype(BF16)

    def seg(lo, width):
        return jnp.dot(h_scr[...], w_ref[:, lo:lo + width], preferred_element_type=F32)

    aq = seg(0, A_WIDTH) * (QK_SCALE * LOG2E)
    ak = seg(A_WIDTH, A_WIDTH)
    av = seg(2 * A_WIDTH, A_WIDTH)
    row_in_blk = lax.broadcasted_iota(jnp.int32, (tm, LANES), 0) % MOBA_BLOCK
    feat_lane = lax.broadcasted_iota(jnp.int32, (tm, LANES), 1)
    feat = jnp.where(feat_lane < ALIBI_PARTS, row_in_blk, 0).astype(F32).astype(BF16)
    for pr in range(A_HEADS // 2):
        ka_ref[:, pr * K_LANES:pr * K_LANES + LANES] = ak[:, pr * LANES:(pr + 1) * LANES].astype(BF16)
        ka_ref[:, pr * K_LANES + LANES:(pr + 1) * K_LANES] = feat
    ga_ref[...] = seg(3 * A_WIDTH, A_WIDTH).astype(BF16)
    for r in range(nblk):
        rows = slice(r * MOBA_BLOCK, (r + 1) * MOBA_BLOCK)
        qT_ref[r] = aq[rows].T.astype(BF16)
        avT = av[rows].T
        for h in range(A_HEADS):
            vT_ref[r, h * V_ROWS:h * V_ROWS + HEAD_DIM, :] = avT[h * HEAD_DIM:(h + 1) * HEAD_DIM].astype(BF16)
            vT_ref[r, h * V_ROWS + HEAD_DIM:(h + 1) * V_ROWS, :] = jnp.ones((V_ROWS - HEAD_DIM, MOBA_BLOCK), BF16)
        kmean_ref[r] = jnp.mean(ak[rows], axis=0, keepdims=True)

    b0 = 4 * A_WIDTH
    cos = cos_ref[...]
    sin = sin_ref[...]
    for t in range(2 * B_WIDTH // LANES):
        v = seg(b0 + t * LANES, LANES)
        rot = v * cos + _swap_half_heads(v) * sin
        if t >= B_WIDTH // LANES:
            rot = rot * QK_SCALE
        ret_ref[:, t * LANES:(t + 1) * LANES] = rot.astype(BF16)
    ret_ref[:, 2 * B_WIDTH:4 * B_WIDTH] = seg(b0 + 2 * B_WIDTH, 2 * B_WIDTH).astype(BF16)

    c0 = b0 + 4 * B_WIDTH

    @pl.when(i == 0)
    def _():
        ext_scr[0:CONV_PAD, :] = jnp.zeros((CONV_PAD, 2 * C_WIDTH), F32)

    ext_scr[CONV_PAD:CONV_PAD + tm, :] = seg(c0, 2 * C_WIDTH)
    y = jnp.zeros((tm, 2 * C_WIDTH), F32) + cb_ref[...]
    for tap in range(CONV_K):
        lo = CONV_PAD - (CONV_K - 1) + tap
        y = y + cw_ref[tap:tap + 1, :] * ext_scr[lo:lo + tm, :]
    ext_scr[0:CONV_PAD, :] = ext_scr[tm:tm + CONV_PAD, :]
    qk = _silu(y)
    mls_ref[:, 0:C_WIDTH] = qk[:, 0:C_WIDTH].astype(BF16)
    mls_ref[:, C_WIDTH:2 * C_WIDTH] = (qk[:, C_WIDTH:] * QK_SCALE).astype(BF16)
    mls_ref[:, 2 * C_WIDTH:5 * C_WIDTH] = seg(c0 + 2 * C_WIDTH, 3 * C_WIDTH).astype(BF16)

    gates_ref[...] = jnp.dot(h_scr[...], wg_ref[...], preferred_element_type=F32)


def _proj(x, norm_w, w_main, w_gate, cos_t, sin_t, conv_w, conv_b):
    s = x.shape[0]
    tm = PROJ_ROWS
    nblk = tm // MOBA_BLOCK
    nb = s // MOBA_BLOCK
    full = lambda shape: pl.BlockSpec(shape, lambda i: (0,) * len(shape))
    return pl.pallas_call(
        _proj_kernel,
        grid=(s // tm,),
        in_specs=[
            pl.BlockSpec((tm, D_MODEL), lambda i: (i, 0)),
            full((1, D_MODEL)),
            full((D_MODEL, N_MAIN)),
            full((D_MODEL, LANES)),
            pl.BlockSpec((tm, LANES), lambda i: (i, 0)),
            pl.BlockSpec((tm, LANES), lambda i: (i, 0)),
            full((CONV_K, 2 * C_WIDTH)),
            full((1, 2 * C_WIDTH)),
        ],
        out_specs=[
            pl.BlockSpec((nblk, A_WIDTH, MOBA_BLOCK), lambda i: (i, 0, 0)),
            pl.BlockSpec((tm, (A_HEADS // 2) * K_LANES), lambda i: (i, 0)),
            pl.BlockSpec((nblk, A_HEADS * V_ROWS, MOBA_BLOCK), lambda i: (i, 0, 0)),
            pl.BlockSpec((tm, A_WIDTH), lambda i: (i, 0)),
            pl.BlockSpec((nblk, 1, A_WIDTH), lambda i: (i, 0, 0)),
            pl.BlockSpec((tm, 4 * B_WIDTH), lambda i: (i, 0)),
            pl.BlockSpec((tm, 5 * C_WIDTH), lambda i: (i, 0)),
            pl.BlockSpec((tm, LANES), lambda i: (i, 0)),
        ],
        out_shape=[
            jax.ShapeDtypeStruct((nb, A_WIDTH, MOBA_BLOCK), BF16),
            jax.ShapeDtypeStruct((s, (A_HEADS // 2) * K_LANES), BF16),
            jax.ShapeDtypeStruct((nb, A_HEADS * V_ROWS, MOBA_BLOCK), BF16),
            jax.ShapeDtypeStruct((s, A_WIDTH), BF16),
            jax.ShapeDtypeStruct((nb, 1, A_WIDTH), F32),
            jax.ShapeDtypeStruct((s, 4 * B_WIDTH), BF16),
            jax.ShapeDtypeStruct((s, 5 * C_WIDTH), BF16),
            jax.ShapeDtypeStruct((s, LANES), F32),
        ],
        scratch_shapes=[
            pltpu.VMEM((tm, D_MODEL), BF16),
            pltpu.VMEM((tm + CONV_PAD, 2 * C_WIDTH), F32),
        ],
        compiler_params=pltpu.CompilerParams(
            dimension_semantics=("arbitrary",), vmem_limit_bytes=VMEM_LIMIT),
        name="proj",
    )(x, norm_w, w_main, w_gate, cos_t, sin_t, conv_w, conv_b)


def _moba_kernel(slopes_ref, alibi_ref, qT_ref, k_ref, vT_ref, kmean_ref, g_ref, o_ref,
                 qm_scr, sel_scr, m_scr, acc_scr, s_scr, gate_scr, own_scr):
    p = pl.program_id(0)
    j = pl.program_id(1)
    blk = MOBA_BLOCK
    nb = kmean_ref.shape[0]

    qT2 = qT_ref[0]
    dim_row = lax.broadcasted_iota(jnp.int32, (LANES, blk), 0)
    blk_row = lax.broadcasted_iota(jnp.int32, (nb, blk), 0)
    key_off = lax.broadcasted_iota(jnp.int32, (blk, blk), 0)
    qry_off = lax.broadcasted_iota(jnp.int32, (blk, blk), 1)
    kmean = kmean_ref[...].astype(BF16)
    k_own = k_ref[pl.ds(pl.multiple_of(j * blk, blk), blk), :]
    vT_own = vT_ref[j]

    def make_group(size, first_block, limit):
        def stage(g):
            base = first_block + g * size
            for b in range(size):
                k_blk = k_ref[pl.ds(pl.multiple_of((base + b) * blk, blk), blk), :]
                for hh in range(2):
                    s_scr[2 * b + hh] = jnp.dot(k_blk, qm_scr[hh], preferred_element_type=F32)

        def process(g):
            base = first_block + g * size
            parts = [[], []]
            for b in range(size):
                n = base + b
                vT_blk = vT_ref[n]
                dist = ((n - j) * blk).astype(F32)
                for hh in range(2):
                    s = s_scr[2 * b + hh]
                    m_loc = jnp.max(s, axis=0, keepdims=True)
                    pm = jnp.exp2(s - m_loc)
                    o_loc = jnp.dot(vT_blk[hh * V_ROWS:(hh + 1) * V_ROWS, :], pm.astype(BF16),
                                    preferred_element_type=F32)
                    chosen = jnp.logical_and(sel_scr[hh, pl.ds(n, 1), :] > 0.0, n < limit)
                    cand = jnp.where(chosen, m_loc + slopes_ref[2 * p + hh] * dist, NEG_INF)
                    parts[hh].append((cand, o_loc))
            for hh in range(2):
                m_old = m_scr[hh]
                m_new = m_old
                for cand, _ in parts[hh]:
                    m_new = jnp.maximum(m_new, cand)
                acc = jnp.exp2(m_old - m_new) * acc_scr[hh]
                for cand, o_loc in parts[hh]:
                    acc = acc + jnp.exp2(cand - m_new) * o_loc
                m_scr[hh] = m_new
                acc_scr[hh] = acc

        def group(g, carry):
            stage(g)
            process(g)
            return carry
        return stage, process, group

    rest = j % MOBA_BIG_GROUP
    stage_short, process_short, short = make_group(MOBA_GROUP, 0, rest)

    for hh in range(2):
        qm = jnp.where((dim_row // HEAD_DIM) == hh, qT2, jnp.zeros_like(qT2))
        coef = jnp.zeros((LANES, blk), F32)
        for part in range(ALIBI_PARTS):
            coef = jnp.where(dim_row == part, alibi_ref[2 * p + hh, part], coef)
        qa = jnp.concatenate([qm, coef.astype(BF16)], axis=0)
        qm_scr[hh] = qa
        gate_scr[hh] = jnp.dot(kmean, qm, preferred_element_type=F32)
        own_scr[hh] = jnp.dot(k_own, qa, preferred_element_type=F32)
    stage_short(0)

    for hh in range(2):
        work = jnp.where(blk_row < j, gate_scr[hh], NEG_INF)
        sel = jnp.zeros((nb, blk), F32)
        for r in range(MOBA_TOPK):
            mx = jnp.max(work, axis=0, keepdims=True)
            idx = jnp.min(jnp.where(work == mx, blk_row, nb), axis=0, keepdims=True)
            pick = blk_row == idx
            sel = jnp.where(jnp.logical_and(pick, r < j), 1.0, sel)
            work = jnp.where(pick, NEG_INF, work)
        sel_scr[hh] = sel

        s = jnp.where(key_off <= qry_off, own_scr[hh], NEG_INF)
        m = jnp.max(s, axis=0, keepdims=True)
        pm = jnp.exp2(s - m)
        m_scr[hh] = m
        acc_scr[hh] = jnp.dot(vT_own[hh * V_ROWS:(hh + 1) * V_ROWS, :], pm.astype(BF16),
                              preferred_element_type=F32)

    process_short(0)
    lax.fori_loop(1, (rest + MOBA_GROUP - 1) // MOBA_GROUP, short, 0)
    n_big = ((j - rest) % MOBA_HUGE_GROUP) // MOBA_BIG_GROUP
    lax.fori_loop(0, n_big, make_group(MOBA_BIG_GROUP, rest, j)[2], 0)
    after_big = rest + n_big * MOBA_BIG_GROUP
    lax.fori_loop(0, (j - after_big) // MOBA_HUGE_GROUP, make_group(MOBA_HUGE_GROUP, after_big, j)[2], 0)

    oT = jnp.concatenate([acc_scr[hh, 0:HEAD_DIM, :] / acc_scr[hh, HEAD_DIM:HEAD_DIM + 1, :] for hh in range(2)],
                         axis=0)
    o_ref[...] = (oT.T * _silu(g_ref[...].astype(F32))).astype(o_ref.dtype)


def _moba(slopes, alibi_parts, qT, ka, vT, kmean, ga):
    s = ka.shape[0]
    nb = s // MOBA_BLOCK
    blk = MOBA_BLOCK
    npair = A_HEADS // 2
    return pl.pallas_call(
        _moba_kernel,
        grid=(npair, nb),
        in_specs=[
            pl.BlockSpec(memory_space=pltpu.SMEM),
            pl.BlockSpec(memory_space=pltpu.SMEM),
            pl.BlockSpec((1, LANES, blk), lambda p, j: (j, p, 0)),
            pl.BlockSpec((s, K_LANES), lambda p, j: (0, p)),
            pl.BlockSpec((nb, 2 * V_ROWS, blk), lambda p, j: (0, p, 0)),
            pl.BlockSpec((nb, LANES), lambda p, j: (0, p)),
            pl.BlockSpec((blk, LANES), lambda p, j: (j, p)),
        ],
        out_specs=pl.BlockSpec((blk, LANES), lambda p, j: (j, p)),
        out_shape=jax.ShapeDtypeStruct((s, A_WIDTH), BF16),
        scratch_shapes=[
            pltpu.VMEM((2, K_LANES, blk), BF16),
            pltpu.VMEM((2, nb, blk), F32),
            pltpu.VMEM((2, 1, blk), F32),
            pltpu.VMEM((2, V_ROWS, blk), F32),
            pltpu.VMEM((2 * MOBA_HUGE_GROUP, blk, blk), F32),
            pltpu.VMEM((2, nb, blk), F32),
            pltpu.VMEM((2, blk, blk), F32),
        ],
        compiler_params=pltpu.CompilerParams(
            dimension_semantics=("arbitrary", "arbitrary"), vmem_limit_bytes=VMEM_LIMIT),
        name="moba",
    )(slopes, alibi_parts, qT, ka, vT, kmean, ga)


def _head_norm_pair(y, lane_head):
    mu = jnp.zeros_like(y)
    for hh in range(2):
        in_h = lane_head == hh
        mu_h = jnp.sum(jnp.where(in_h, y, 0.0), axis=-1, keepdims=True) * (1.0 / HEAD_DIM)
        mu = jnp.where(in_h, mu_h, mu)
    d = y - mu
    var = jnp.zeros_like(y)
    for hh in range(2):
        in_h = lane_head == hh
        var_h = jnp.sum(jnp.where(in_h, d * d, 0.0), axis=-1, keepdims=True) * (1.0 / HEAD_DIM)
        var = jnp.where(in_h, var_h, var)
    return d * lax.rsqrt(var + HEAD_NORM_EPS)


def _ret_kernel(x_ref, decay_ref, zeta_ref, xi_ref, cd_ref, nw_ref, o_ref, state_scr, qk_scr):
    step = pl.program_id(0)
    L = RET_CHUNK
    ntile = B_WIDTH // LANES

    @pl.when(step == 0)
    def _():
        state_scr[...] = jnp.zeros_like(state_scr)

    lane_head = lax.broadcasted_iota(jnp.int32, (L, LANES), 1) // HEAD_DIM
    sq_row = lax.broadcasted_iota(jnp.int32, (LANES, LANES), 0) // HEAD_DIM
    sq_col = lax.broadcasted_iota(jnp.int32, (LANES, LANES), 1) // HEAD_DIM

    for c in range(RET_STEP_CHUNKS):
        rows = slice(c * L, (c + 1) * L)
        for t in range(ntile):
            q2 = x_ref[rows, t * LANES:(t + 1) * LANES]
            k2 = x_ref[rows, B_WIDTH + t * LANES:B_WIDTH + (t + 1) * LANES]
            for hh in range(2):
                qm = jnp.where(lane_head == hh, q2, jnp.zeros_like(q2))
                qk_scr[(c * ntile + t) * 2 + hh] = lax.dot_general(
                    qm, k2, (((1,), (1,)), ((), ())), preferred_element_type=F32)

    for t in range(ntile):
        cols = slice(t * LANES, (t + 1) * LANES)
        state = state_scr[t]
        for c in range(RET_STEP_CHUNKS):
            rows = slice(c * L, (c + 1) * L)
            q2 = x_ref[rows, t * LANES:(t + 1) * LANES]
            k2 = x_ref[rows, B_WIDTH + t * LANES:B_WIDTH + (t + 1) * LANES]
            v2 = x_ref[rows, 2 * B_WIDTH + t * LANES:2 * B_WIDTH + (t + 1) * LANES]
            g2 = x_ref[rows, 3 * B_WIDTH + t * LANES:3 * B_WIDTH + (t + 1) * LANES]
            y = jnp.dot(q2, state.astype(BF16), preferred_element_type=F32) * xi_ref[:, cols]
            for hh in range(2):
                sc = qk_scr[(c * ntile + t) * 2 + hh] * decay_ref[2 * t + hh]
                o_h = jnp.dot(sc.astype(BF16), v2, preferred_element_type=F32)
                y = y + jnp.where(lane_head == hh, o_h, 0.0)
            kz = (k2.astype(F32) * zeta_ref[:, cols]).astype(BF16)
            kv = lax.dot_general(kz, v2, (((0,), (0,)), ((), ())), preferred_element_type=F32)
            state = cd_ref[t] * state + jnp.where(sq_row == sq_col, kv, 0.0)
            yn = _head_norm_pair(y, lane_head) * nw_ref[:, cols]
            o_ref[rows, cols] = (yn * _silu(g2.astype(F32))).astype(o_ref.dtype)
        state_scr[t] = state


def _ret(ret_in, decay, zeta, xi, cd, norm_w):
    s = ret_in.shape[0]
    L = RET_CHUNK
    rows = L * RET_STEP_CHUNKS
    ntile = B_WIDTH // LANES
    full = lambda shape: pl.BlockSpec(shape, lambda c: (0,) * len(shape))
    return pl.pallas_call(
        _ret_kernel,
        grid=(s // rows,),
        in_specs=[
            pl.BlockSpec((rows, 4 * B_WIDTH), lambda c: (c, 0)),
            full((B_HEADS, L, L)),
            full((L, B_WIDTH)),
            full((L, B_WIDTH)),
            full((ntile, LANES, LANES)),
            full((1, B_WIDTH)),
        ],
        out_specs=pl.BlockSpec((rows, B_WIDTH), lambda c: (c, 0)),
        out_shape=jax.ShapeDtypeStruct((s, B_WIDTH), BF16),
        scratch_shapes=[
            pltpu.VMEM((ntile, LANES, LANES), F32),
            pltpu.VMEM((RET_STEP_CHUNKS * ntile * 2, L, L), F32),
        ],
        compiler_params=pltpu.CompilerParams(
            dimension_semantics=("arbitrary",), vmem_limit_bytes=VMEM_LIMIT),
        name="ret",
    )(ret_in, decay, zeta, xi, cd, norm_w)


def _mlstm_kernel(x_ref, gates_ref, gb_ref, nw_ref, o_ref, st_scr, m_scr, qk_scr):
    step = pl.program_id(0)
    L = MLSTM_CHUNK
    ntile = C_WIDTH // LANES

    @pl.when(step == 0)
    def _():
        st_scr[...] = jnp.zeros_like(st_scr)
        m_scr[...] = jnp.zeros_like(m_scr)

    lane_head = lax.broadcasted_iota(jnp.int32, (L, LANES), 1) // HEAD_DIM
    lane_head1 = lane_head[0:1, :]
    sq_row = lax.broadcasted_iota(jnp.int32, (LANES, 2 * LANES), 0) // HEAD_DIM
    sq_col = (lax.broadcasted_iota(jnp.int32, (LANES, 2 * LANES), 1) % LANES) // HEAD_DIM
    same_head = sq_row == sq_col
    t_row = lax.broadcasted_iota(jnp.int32, (L, L), 0)
    t_col = lax.broadcasted_iota(jnp.int32, (L, L), 1)
    causal = t_row >= t_col
    lower = causal.astype(F32)
    ones_tile = jnp.ones((L, LANES), BF16)

    for c in range(MLSTM_STEP_CHUNKS):
        rows = slice(c * L, (c + 1) * L)
        for t in range(ntile):
            q2 = x_ref[rows, t * LANES:(t + 1) * LANES]
            k2 = x_ref[rows, C_WIDTH + t * LANES:C_WIDTH + (t + 1) * LANES]
            for hh in range(2):
                qm = jnp.where(lane_head == hh, q2, jnp.zeros_like(q2))
                qk_scr[(c * ntile + t) * 2 + hh] = lax.dot_general(
                    qm, k2, (((1,), (1,)), ((), ())), preferred_element_type=F32)

    local = []
    for c in range(MLSTM_STEP_CHUNKS):
        rows = slice(c * L, (c + 1) * L)
        pre = gates_ref[rows, :] + gb_ref[...]
        logsig = jnp.minimum(pre, 0.0) - jnp.log1p(jnp.exp(-jnp.abs(pre)))
        bsum = jnp.dot(lower, logsig, preferred_element_type=F32, precision=lax.Precision.HIGHEST)
        pre_t = pre.T
        bsum_t = bsum.T
        for t in range(ntile):
            k2 = x_ref[rows, C_WIDTH + t * LANES:C_WIDTH + (t + 1) * LANES]
            v2 = x_ref[rows, 2 * C_WIDTH + t * LANES:2 * C_WIDTH + (t + 1) * LANES]
            v_aug = jnp.concatenate([v2, ones_tile], axis=1)
            num_l = jnp.zeros((L, LANES), F32)
            den_l = jnp.zeros((L, LANES), F32)
            mloc_l = jnp.zeros((L, LANES), F32)
            b_l = jnp.zeros((L, LANES), F32)
            wg_l = jnp.zeros((L, LANES), F32)
            gmax_l = jnp.zeros((1, LANES), F32)
            blast_l = jnp.zeros((1, LANES), F32)
            for hh in range(2):
                h = 2 * t + hh
                in_h = lane_head == hh
                in_h1 = lane_head1 == hh
                i_col = pre[:, h:h + 1]
                b_col = bsum[:, C_HEADS + h:C_HEADS + h + 1]
                i_row = pre_t[h:h + 1, :]
                b_row = bsum_t[C_HEADS + h:C_HEADS + h + 1, :]
                d_log = jnp.where(causal, b_col - b_row + i_row, NEG_INF)
                m_loc = jnp.max(d_log, axis=-1, keepdims=True)
                sc = qk_scr[(c * ntile + t) * 2 + hh] * jnp.exp(d_log - m_loc)
                nd = jnp.dot(sc.astype(BF16), v_aug, preferred_element_type=F32)
                num_l = jnp.where(in_h, nd[:, 0:LANES], num_l)
                den_l = jnp.where(in_h, nd[:, LANES:2 * LANES], den_l)
                mloc_l = jnp.where(in_h, m_loc, mloc_l)
                b_l = jnp.where(in_h, b_col, b_l)
                b_last = b_col[L - 1:L, :]
                g_col = b_last - b_col + i_col
                g_max = jnp.max(g_col, axis=0, keepdims=True)
                wg_l = jnp.where(in_h, jnp.exp(g_col - g_max), wg_l)
                gmax_l = jnp.where(in_h1, g_max, gmax_l)
                blast_l = jnp.where(in_h1, b_last, blast_l)
            kw = (k2.astype(F32) * wg_l).astype(BF16)
            kv = lax.dot_general(kw, v_aug, (((0,), (0,)), ((), ())), preferred_element_type=F32)
            kv = jnp.where(same_head, kv, 0.0)
            local.append((num_l, den_l, mloc_l, b_l, gmax_l, blast_l, kv))

    for t in range(ntile):
        cols = slice(t * LANES, (t + 1) * LANES)
        state = st_scr[t]
        m_st = m_scr[t]
        for c in range(MLSTM_STEP_CHUNKS):
            rows = slice(c * L, (c + 1) * L)
            num_l, den_l, mloc_l, b_l, gmax_l, blast_l, kv = local[c * ntile + t]
            q2 = x_ref[rows, t * LANES:(t + 1) * LANES]
            o2 = x_ref[rows, 3 * C_WIDTH + t * LANES:3 * C_WIDTH + (t + 1) * LANES]
            g2 = x_ref[rows, 4 * C_WIDTH + t * LANES:4 * C_WIDTH + (t + 1) * LANES]
            qs = jnp.dot(q2, state.astype(BF16), preferred_element_type=F32)
            inter_log = b_l + m_st
            m_t = jnp.maximum(inter_log, mloc_l)
            r_intra = jnp.exp(mloc_l - m_t)
            w_inter = jnp.exp(inter_log - m_t)
            num = r_intra * num_l + w_inter * qs[:, 0:LANES]
            den = r_intra * den_l + w_inter * qs[:, LANES:2 * LANES]
            h_t = num / jnp.maximum(jnp.abs(den), jnp.exp(-m_t))
            m_new = jnp.maximum(blast_l + m_st, gmax_l)
            a = jnp.exp(blast_l + m_st - m_new)
            beta = jnp.exp(gmax_l - m_new)
            a2 = jnp.concatenate([a, a], axis=1)
            beta2 = jnp.concatenate([beta, beta], axis=1)
            state = a2 * state + beta2 * kv
            m_st = m_new
            y = h_t * jax.nn.sigmoid(o2.astype(F32))
            yn = _head_norm_pair(y, lane_head) * nw_ref[:, cols]
            o_ref[rows, cols] = (yn * _silu(g2.astype(F32))).astype(o_ref.dtype)
        st_scr[t] = state
        m_scr[t] = m_st


def _mlstm(mls_in, gates, gate_bias, norm_w):
    s = mls_in.shape[0]
    rows = MLSTM_CHUNK * MLSTM_STEP_CHUNKS
    npair = C_WIDTH // LANES
    full = lambda shape: pl.BlockSpec(shape, lambda c: (0,) * len(shape))
    return pl.pallas_call(
        _mlstm_kernel,
        grid=(s // rows,),
        in_specs=[
            pl.BlockSpec((rows, 5 * C_WIDTH), lambda c: (c, 0)),
            pl.BlockSpec((rows, LANES), lambda c: (c, 0)),
            full((1, LANES)),
            full((1, C_WIDTH)),
        ],
        out_specs=pl.BlockSpec((rows, C_WIDTH), lambda c: (c, 0)),
        out_shape=jax.ShapeDtypeStruct((s, C_WIDTH), BF16),
        scratch_shapes=[
            pltpu.VMEM((npair, LANES, 2 * LANES), F32),
            pltpu.VMEM((npair, 1, LANES), F32),
            pltpu.VMEM((MLSTM_STEP_CHUNKS * npair * 2, MLSTM_CHUNK, MLSTM_CHUNK), F32),
        ],
        compiler_params=pltpu.CompilerParams(
            dimension_semantics=("arbitrary",), vmem_limit_bytes=VMEM_LIMIT),
        name="mlstm",
    )(mls_in, gates, gate_bias, norm_w)


def _outproj_kernel(x_ref, ya_ref, yb_ref, yc_ref, wa_ref, wb_ref, wc_ref, fw_ref, o_ref, *, final_norm):
    acc = x_ref[...]
    acc = acc + jnp.dot(ya_ref[...], wa_ref[...], preferred_element_type=F32)
    acc = acc + jnp.dot(yb_ref[...], wb_ref[...], preferred_element_type=F32)
    acc = acc + jnp.dot(yc_ref[...], wc_ref[...], preferred_element_type=F32)
    if final_norm:
        ms = jnp.mean(acc * acc, axis=-1, keepdims=True)
        acc = (acc * lax.rsqrt(ms + RMS_EPS)) * fw_ref[...]
    o_ref[...] = acc


def _outproj(x, ya, yb, yc, wa, wb, wc, final_w, final_norm):
    s = x.shape[0]
    tm = OUT_ROWS
    full = lambda shape: pl.BlockSpec(shape, lambda i: (0,) * len(shape))
    rows = lambda width: pl.BlockSpec((tm, width), lambda i: (i, 0))
    return pl.pallas_call(
        functools.partial(_outproj_kernel, final_norm=final_norm),
        grid=(s // tm,),
        in_specs=[rows(D_MODEL), rows(A_WIDTH), rows(B_WIDTH), rows(C_WIDTH),
                  full((A_WIDTH, D_MODEL)), full((B_WIDTH, D_MODEL)), full((C_WIDTH, D_MODEL)),
                  full((1, D_MODEL))],
        out_specs=rows(D_MODEL),
        out_shape=jax.ShapeDtypeStruct((s, D_MODEL), F32),
        compiler_params=pltpu.CompilerParams(
            dimension_semantics=("arbitrary",), vmem_limit_bytes=VMEM_LIMIT),
        name="outproj",
    )(x, ya, yb, yc, wa, wb, wc, final_w)


def _rope_tables(s):
    half = HEAD_DIM // 2
    pos = jnp.arange(s, dtype=F32)
    theta = 1.0 / (ROPE_BASE ** jnp.linspace(0.0, 1.0, half, dtype=F32))
    ang = pos[:, None] * theta[None, :]
    cos, sin = jnp.cos(ang), jnp.sin(ang)
    cos_t = jnp.tile(cos, (1, LANES // half))
    sin_t = jnp.tile(jnp.concatenate([-sin, sin], axis=-1), (1, LANES // HEAD_DIM))
    return cos_t, sin_t


def _retention_tables():
    L = RET_CHUNK
    h = B_HEADS
    log_gamma = jnp.log(1.0 - jnp.exp2(-5.0 - jnp.arange(h, dtype=F32)))
    pos = jnp.arange(L, dtype=F32)
    diff = pos[:, None] - pos[None, :]
    decay = jnp.where(diff >= 0, jnp.exp(log_gamma[:, None, None] * jnp.maximum(diff, 0.0)), 0.0)
    zeta = jnp.exp(log_gamma[:, None] * (L - 1.0 - pos))
    xi = jnp.exp(log_gamma[:, None] * (pos + 1.0))
    chunk_decay = jnp.exp(log_gamma * L)
    per_lane = lambda t: jnp.repeat(t.T, HEAD_DIM, axis=1)
    cd_lane = jnp.repeat(chunk_decay, HEAD_DIM).reshape(B_WIDTH // LANES, LANES, 1)
    cd = jnp.broadcast_to(cd_lane, (B_WIDTH // LANES, LANES, LANES))
    return decay, per_lane(zeta), per_lane(xi), cd


def _alibi_slopes():
    return jnp.exp2(-8.0 * jnp.arange(1, A_HEADS + 1, dtype=F32) / A_HEADS) * LOG2E


def _split_bf16(t):
    parts, rest = [], t
    for _ in range(ALIBI_PARTS):
        piece = rest.astype(BF16).astype(F32)
        parts.append(piece)
        rest = rest - piece
    return jnp.stack(parts, axis=-1)


def kernel(x, norm_w, w_in, conv_w, conv_b, gate_bias, ret_norm_w, mlstm_norm_w, w_out, final_norm_w):
    b, s, d = x.shape
    assert b == 1 and d == D_MODEL and s % PROJ_ROWS == 0 and s % OUT_ROWS == 0
    assert s % (MLSTM_CHUNK * MLSTM_STEP_CHUNKS) == 0 and s % (RET_CHUNK * RET_STEP_CHUNKS) == 0
    depth = w_in.shape[0]
    cos_t, sin_t = _rope_tables(s)
    decay, zeta, xi, cd = _retention_tables()
    slopes = _alibi_slopes()
    alibi_parts = _split_bf16(slopes)

    w_main = w_in[:, :, :N_MAIN].astype(BF16)
    w_gate = jnp.pad(w_in[:, :, N_MAIN:], ((0, 0), (0, 0), (0, LANES - 2 * C_HEADS)))
    w_gate = w_gate.astype(BF16)
    w_out_b = w_out.astype(BF16)
    gb = jnp.pad(gate_bias.astype(F32), ((0, 0), (0, LANES - 2 * C_HEADS)))

    xs = x[0]
    for layer in range(depth):
        qT, ka, vT, ga, kmean, ret_in, mls_in, gates = _proj(
            xs, norm_w[layer][None, :], w_main[layer], w_gate[layer], cos_t, sin_t,
            conv_w[layer], conv_b[layer][None, :])
        ya = _moba(slopes, alibi_parts, qT, ka, vT, kmean.reshape(-1, A_WIDTH), ga)
        yb = _ret(ret_in, decay, zeta, xi, cd, ret_norm_w[layer][None, :])
        yc = _mlstm(mls_in, gates, gb[layer][None, :], mlstm_norm_w[layer][None, :])
        wo = w_out_b[layer]
        xs = _outproj(xs, ya, yb, yc, wo[:A_WIDTH], wo[A_WIDTH:A_WIDTH + B_WIDTH], wo[A_WIDTH + B_WIDTH:],
                      final_norm_w[None, :], final_norm=(layer == depth - 1))
    return xs[None]
```

```python
import functools

import numpy as np
import jax
import jax.numpy as jnp
from jax import lax
from jax.experimental import pallas as pl
from jax.experimental.pallas import tpu as pltpu

D_MODEL = 1024
HEAD_DIM = 64
A_HEADS, B_HEADS, C_HEADS = 8, 4, 4
A_WIDTH, B_WIDTH, C_WIDTH = A_HEADS * HEAD_DIM, B_HEADS * HEAD_DIM, C_HEADS * HEAD_DIM
MOBA_BLOCK = 256
MOBA_TOPK = 3
MOBA_GROUP = 4
MOBA_BIG_GROUP = 16
MOBA_HUGE_GROUP = 32
RET_CHUNK = 256
RET_STEP_CHUNKS = 4
MLSTM_CHUNK = 128
MLSTM_STEP_CHUNKS = 4
CONV_K = 4
RMS_EPS = 1e-6
HEAD_NORM_EPS = 1e-5
ROPE_BASE = 10000.0
N_MAIN = 4 * A_WIDTH + 4 * B_WIDTH + 5 * C_WIDTH
QK_SCALE = HEAD_DIM ** -0.5
V_ROWS = HEAD_DIM + 16
K_LANES = 2 * 128
ALIBI_PARTS = 3
LOG2E = 1.4426950408889634

LANES = 128
CONV_PAD = 8
PROJ_ROWS = 512
OUT_ROWS = 512
VMEM_LIMIT = 56 * 1024 * 1024

BF16 = jnp.bfloat16
F32 = jnp.float32
NEG_INF = float("-inf")


def _silu(t):
    return t * jax.nn.sigmoid(t)


def _swap_half_heads(t):
    lane = lax.broadcasted_iota(jnp.int32, t.shape, 1)
    first_half = (lane % HEAD_DIM) < (HEAD_DIM // 2)
    return jnp.where(first_half, pltpu.roll(t, LANES - HEAD_DIM // 2, 1), pltpu.roll(t, HEAD_DIM // 2, 1))


def _proj_kernel(x_ref, nw_ref, w_ref, wg_ref, cos_ref, sin_ref, cw_ref, cb_ref,
                 qT_ref, ka_ref, vT_ref, ga_ref, kmean_ref, ret_ref, mls_ref, gates_ref,
                 h_scr, ext_scr):
    i = pl.program_id(0)
    tm = x_ref.shape[0]
    nblk = tm // MOBA_BLOCK

    x = x_ref[...]
    ms = jnp.mean(x * x, axis=-1, keepdims=True)
    h_scr[...] = ((x * lax.rsqrt(ms + RMS_EPS)) * nw_ref[...]).astype(BF16)

    def seg(lo, width):
        return jnp.dot(h_scr[...], w_ref[:, lo:lo + width], preferred_element_type=F32)

    aq = seg(0, A_WIDTH) * (QK_SCALE * LOG2E)
    ak = seg(A_WIDTH, A_WIDTH)
    av = seg(2 * A_WIDTH, A_WIDTH)
    row_in_blk = lax.broadcasted_iota(jnp.int32, (tm, LANES), 0) % MOBA_BLOCK
    feat_lane = lax.broadcasted_iota(jnp.int32, (tm, LANES), 1)
    feat = jnp.where(feat_lane < ALIBI_PARTS, row_in_blk, 0).astype(F32).astype(BF16)
    for pr in range(A_HEADS // 2):
        ka_ref[:, pr * K_LANES:pr * K_LANES + LANES] = ak[:, pr * LANES:(pr + 1) * LANES].astype(BF16)
        ka_ref[:, pr * K_LANES + LANES:(pr + 1) * K_LANES] = feat
    ga_ref[...] = seg(3 * A_WIDTH, A_WIDTH).astype(BF16)
    for r in range(nblk):
        rows = slice(r * MOBA_BLOCK, (r + 1) * MOBA_BLOCK)
        qT_ref[r] = aq[rows].T.astype(BF16)
        avT = av[rows].T
        for h in range(A_HEADS):
            vT_ref[r, h * V_ROWS:h * V_ROWS + HEAD_DIM, :] = avT[h * HEAD_DIM:(h + 1) * HEAD_DIM].astype(BF16)
            vT_ref[r, h * V_ROWS + HEAD_DIM:(h + 1) * V_ROWS, :] = jnp.ones((V_ROWS - HEAD_DIM, MOBA_BLOCK), BF16)
        kmean_ref[r] = jnp.mean(ak[rows], axis=0, keepdims=True)

    b0 = 4 * A_WIDTH
    cos = cos_ref[...]
    sin = sin_ref[...]
    bqk = seg(b0, 2 * B_WIDTH)
    for t in range(2 * B_WIDTH // LANES):
        v = bqk[:, t * LANES:(t + 1) * LANES]
        rot = v * cos + _swap_half_heads(v) * sin
        if t >= B_WIDTH // LANES:
            rot = rot * QK_SCALE
        ret_ref[:, t * LANES:(t + 1) * LANES] = rot.astype(BF16)
    ret_ref[:, 2 * B_WIDTH:4 * B_WIDTH] = seg(b0 + 2 * B_WIDTH, 2 * B_WIDTH).astype(BF16)

    c0 = b0 + 4 * B_WIDTH

    @pl.when(i == 0)
    def _():
        ext_scr[0:CONV_PAD, :] = jnp.zeros((CONV_PAD, 2 * C_WIDTH), F32)

    ext_scr[CONV_PAD:CONV_PAD + tm, :] = seg(c0, 2 * C_WIDTH)
    y = jnp.zeros((tm, 2 * C_WIDTH), F32) + cb_ref[...]
    for tap in range(CONV_K):
        lo = CONV_PAD - (CONV_K - 1) + tap
        y = y + cw_ref[tap:tap + 1, :] * ext_scr[lo:lo + tm, :]
    ext_scr[0:CONV_PAD, :] = ext_scr[tm:tm + CONV_PAD, :]
    qk = _silu(y)
    mls_ref[:, 0:C_WIDTH] = qk[:, 0:C_WIDTH].astype(BF16)
    mls_ref[:, C_WIDTH:2 * C_WIDTH] = (qk[:, C_WIDTH:] * QK_SCALE).astype(BF16)
    mls_ref[:, 2 * C_WIDTH:5 * C_WIDTH] = seg(c0 + 2 * C_WIDTH, 3 * C_WIDTH).astype(BF16)

    gates_ref[...] = jnp.dot(h_scr[...], wg_ref[...], preferred_element_type=F32)


def _proj(x, norm_w, w_main, w_gate, cos_t, sin_t, conv_w, conv_b):
    s = x.shape[0]
    tm = PROJ_ROWS
    nblk = tm // MOBA_BLOCK
    nb = s // MOBA_BLOCK
    full = lambda shape: pl.BlockSpec(shape, lambda i: (0,) * len(shape))
    return pl.pallas_call(
        _proj_kernel,
        grid=(s // tm,),
        in_specs=[
            pl.BlockSpec((tm, D_MODEL), lambda i: (i, 0)),
            full((1, D_MODEL)),
            full((D_MODEL, N_MAIN)),
            full((D_MODEL, LANES)),
            pl.BlockSpec((tm, LANES), lambda i: (i, 0)),
            pl.BlockSpec((tm, LANES), lambda i: (i, 0)),
            full((CONV_K, 2 * C_WIDTH)),
            full((1, 2 * C_WIDTH)),
        ],
        out_specs=[
            pl.BlockSpec((nblk, A_WIDTH, MOBA_BLOCK), lambda i: (i, 0, 0)),
            pl.BlockSpec((tm, (A_HEADS // 2) * K_LANES), lambda i: (i, 0)),
            pl.BlockSpec((nblk, A_HEADS * V_ROWS, MOBA_BLOCK), lambda i: (i, 0, 0)),
            pl.BlockSpec((tm, A_WIDTH), lambda i: (i, 0)),
            pl.BlockSpec((nblk, 1, A_WIDTH), lambda i: (i, 0, 0)),
            pl.BlockSpec((tm, 4 * B_WIDTH), lambda i: (i, 0)),
            pl.BlockSpec((tm, 5 * C_WIDTH), lambda i: (i, 0)),
            pl.BlockSpec((tm, LANES), lambda i: (i, 0)),
        ],
        out_shape=[
            jax.ShapeDtypeStruct((nb, A_WIDTH, MOBA_BLOCK), BF16),
            jax.ShapeDtypeStruct((s, (A_HEADS // 2) * K_LANES), BF16),
            jax.ShapeDtypeStruct((nb, A_HEADS * V_ROWS, MOBA_BLOCK), BF16),
            jax.ShapeDtypeStruct((s, A_WIDTH), BF16),
            jax.ShapeDtypeStruct((nb, 1, A_WIDTH), F32),
            jax.ShapeDtypeStruct((s, 4 * B_WIDTH), BF16),
            jax.ShapeDtypeStruct((s, 5 * C_WIDTH), BF16),
            jax.ShapeDtypeStruct((s, LANES), F32),
        ],
        scratch_shapes=[
            pltpu.VMEM((tm, D_MODEL), BF16),
            pltpu.VMEM((tm + CONV_PAD, 2 * C_WIDTH), F32),
        ],
        compiler_params=pltpu.CompilerParams(
            dimension_semantics=("arbitrary",), vmem_limit_bytes=VMEM_LIMIT),
        name="proj",
    )(x, norm_w, w_main, w_gate, cos_t, sin_t, conv_w, conv_b)


def _moba_kernel(slopes_ref, alibi_ref, qT_ref, k_ref, vT_ref, kmean_ref, g_ref, o_ref,
                 qm_scr, sel_scr, m_scr, acc_scr, s_scr, mloc_scr, gate_scr, own_scr):
    p = pl.program_id(0)
    j = pl.program_id(1)
    blk = MOBA_BLOCK
    nb = kmean_ref.shape[0]

    qT2 = qT_ref[0]
    dim_row = lax.broadcasted_iota(jnp.int32, (LANES, blk), 0)
    blk_row = lax.broadcasted_iota(jnp.int32, (nb, blk), 0)
    key_off = lax.broadcasted_iota(jnp.int32, (blk, blk), 0)
    qry_off = lax.broadcasted_iota(jnp.int32, (blk, blk), 1)
    kmean = kmean_ref[...].astype(BF16)
    k_own = k_ref[pl.ds(pl.multiple_of(j * blk, blk), blk), :]
    vT_own = vT_ref[j]

    def make_group(size, first_block, limit):
        def stage(g):
            base = first_block + g * size
            for b in range(size):
                k_blk = k_ref[pl.ds(pl.multiple_of((base + b) * blk, blk), blk), :]
                for hh in range(2):
                    sc = jnp.dot(k_blk, qm_scr[hh], preferred_element_type=F32)
                    s_scr[2 * b + hh] = sc
                    mloc_scr[2 * b + hh] = jnp.max(sc, axis=0, keepdims=True)

        def process(g):
            base = first_block + g * size
            parts = [[], []]
            for b in range(size):
                n = base + b
                vT_blk = vT_ref[n]
                dist = ((n - j) * blk).astype(F32)
                for hh in range(2):
                    m_loc = mloc_scr[2 * b + hh]
                    pm = jnp.exp2(s_scr[2 * b + hh] - m_loc)
                    o_loc = jnp.dot(vT_blk[hh * V_ROWS:(hh + 1) * V_ROWS, :], pm.astype(BF16),
                                    preferred_element_type=F32)
                    chosen = jnp.logical_and(sel_scr[hh, pl.ds(n, 1), :] > 0.0, n < limit)
                    cand = jnp.where(chosen, m_loc + slopes_ref[2 * p + hh] * dist, NEG_INF)
                    parts[hh].append((cand, o_loc))
            for hh in range(2):
                m_old = m_scr[hh]
                m_new = m_old
                for cand, _ in parts[hh]:
                    m_new = jnp.maximum(m_new, cand)
                acc = jnp.exp2(m_old - m_new) * acc_scr[hh]
                for cand, o_loc in parts[hh]:
                    acc = acc + jnp.exp2(cand - m_new) * o_loc
                m_scr[hh] = m_new
                acc_scr[hh] = acc

        def group(g, carry):
            stage(g)
            process(g)
            return carry
        return stage, process, group

    rest = j % MOBA_BIG_GROUP
    stage_short, process_short, short = make_group(MOBA_GROUP, 0, rest)

    for hh in range(2):
        qm = jnp.where((dim_row // HEAD_DIM) == hh, qT2, jnp.zeros_like(qT2))
        coef = jnp.zeros((LANES, blk), F32)
        for part in range(ALIBI_PARTS):
            coef = jnp.where(dim_row == part, alibi_ref[2 * p + hh, part], coef)
        qa = jnp.concatenate([qm, coef.astype(BF16)], axis=0)
        qm_scr[hh] = qa
        gate_scr[hh] = jnp.dot(kmean, qm, preferred_element_type=F32)
        own_scr[hh] = jnp.dot(k_own, qa, preferred_element_type=F32)
    stage_short(0)

    for hh in range(2):
        work = jnp.where(blk_row < j, gate_scr[hh], NEG_INF)
        sel = jnp.zeros((nb, blk), F32)
        for r in range(MOBA_TOPK):
            mx = jnp.max(work, axis=0, keepdims=True)
            idx = jnp.min(jnp.where(work == mx, blk_row, nb), axis=0, keepdims=True)
            pick = blk_row == idx
            sel = jnp.where(jnp.logical_and(pick, r < j), 1.0, sel)
            work = jnp.where(pick, NEG_INF, work)
        sel_scr[hh] = sel

        s = jnp.where(key_off <= qry_off, own_scr[hh], NEG_INF)
        m = jnp.max(s, axis=0, keepdims=True)
        pm = jnp.exp2(s - m)
        m_scr[hh] = m
        acc_scr[hh] = jnp.dot(vT_own[hh * V_ROWS:(hh + 1) * V_ROWS, :], pm.astype(BF16),
                              preferred_element_type=F32)

    process_short(0)
    lax.fori_loop(1, (rest + MOBA_GROUP - 1) // MOBA_GROUP, short, 0)
    n_big = ((j - rest) % MOBA_HUGE_GROUP) // MOBA_BIG_GROUP
    lax.fori_loop(0, n_big, make_group(MOBA_BIG_GROUP, rest, j)[2], 0)
    after_big = rest + n_big * MOBA_BIG_GROUP
    lax.fori_loop(0, (j - after_big) // MOBA_HUGE_GROUP, make_group(MOBA_HUGE_GROUP, after_big, j)[2], 0)

    oT = jnp.concatenate([acc_scr[hh, 0:HEAD_DIM, :] / acc_scr[hh, HEAD_DIM:HEAD_DIM + 1, :] for hh in range(2)],
                         axis=0)
    o_ref[...] = (oT.T * _silu(g_ref[...].astype(F32))).astype(o_ref.dtype)


def _moba(slopes, alibi_parts, qT, ka, vT, kmean, ga):
    s = ka.shape[0]
    nb = s // MOBA_BLOCK
    blk = MOBA_BLOCK
    npair = A_HEADS // 2
    return pl.pallas_call(
        _moba_kernel,
        grid=(npair, nb),
        in_specs=[
            pl.BlockSpec(memory_space=pltpu.SMEM),
            pl.BlockSpec(memory_space=pltpu.SMEM),
            pl.BlockSpec((1, LANES, blk), lambda p, j: (j, p, 0)),
            pl.BlockSpec((s, K_LANES), lambda p, j: (0, p)),
            pl.BlockSpec((nb, 2 * V_ROWS, blk), lambda p, j: (0, p, 0)),
            pl.BlockSpec((nb, LANES), lambda p, j: (0, p)),
            pl.BlockSpec((blk, LANES), lambda p, j: (j, p)),
        ],
        out_specs=pl.BlockSpec((blk, LANES), lambda p, j: (j, p)),
        out_shape=jax.ShapeDtypeStruct((s, A_WIDTH), BF16),
        scratch_shapes=[
            pltpu.VMEM((2, K_LANES, blk), BF16),
            pltpu.VMEM((2, nb, blk), F32),
            pltpu.VMEM((2, 1, blk), F32),
            pltpu.VMEM((2, V_ROWS, blk), F32),
            pltpu.VMEM((2 * MOBA_HUGE_GROUP, blk, blk), F32),
            pltpu.VMEM((2 * MOBA_HUGE_GROUP, 1, blk), F32),
            pltpu.VMEM((2, nb, blk), F32),
            pltpu.VMEM((2, blk, blk), F32),
        ],
        compiler_params=pltpu.CompilerParams(
            dimension_semantics=("arbitrary", "arbitrary"), vmem_limit_bytes=VMEM_LIMIT),
        name="moba",
    )(slopes, alibi_parts, qT, ka, vT, kmean, ga)


def _head_norm_pair(y, lane_head):
    mu = jnp.zeros_like(y)
    for hh in range(2):
        in_h = lane_head == hh
        mu_h = jnp.sum(jnp.where(in_h, y, 0.0), axis=-1, keepdims=True) * (1.0 / HEAD_DIM)
        mu = jnp.where(in_h, mu_h, mu)
    d = y - mu
    var = jnp.zeros_like(y)
    for hh in range(2):
        in_h = lane_head == hh
        var_h = jnp.sum(jnp.where(in_h, d * d, 0.0), axis=-1, keepdims=True) * (1.0 / HEAD_DIM)
        var = jnp.where(in_h, var_h, var)
    return d * lax.rsqrt(var + HEAD_NORM_EPS)


def _ret_kernel(x_ref, decay_ref, zeta_ref, xi_ref, cd_ref, nw_ref, o_ref, state_scr, qk_scr):
    step = pl.program_id(0)
    L = RET_CHUNK
    ntile = B_WIDTH // LANES

    @pl.when(step == 0)
    def _():
        state_scr[...] = jnp.zeros_like(state_scr)

    lane_head = lax.broadcasted_iota(jnp.int32, (L, LANES), 1) // HEAD_DIM
    sq_row = lax.broadcasted_iota(jnp.int32, (LANES, LANES), 0) // HEAD_DIM
    sq_col = lax.broadcasted_iota(jnp.int32, (LANES, LANES), 1) // HEAD_DIM

    for c in range(RET_STEP_CHUNKS):
        rows = slice(c * L, (c + 1) * L)
        for t in range(ntile):
            q2 = x_ref[rows, t * LANES:(t + 1) * LANES]
            k2 = x_ref[rows, B_WIDTH + t * LANES:B_WIDTH + (t + 1) * LANES]
            for hh in range(2):
                qm = jnp.where(lane_head == hh, q2, jnp.zeros_like(q2))
                qk_scr[(c * ntile + t) * 2 + hh] = lax.dot_general(
                    qm, k2, (((1,), (1,)), ((), ())), preferred_element_type=F32)

    for t in range(ntile):
        cols = slice(t * LANES, (t + 1) * LANES)
        state = state_scr[t]
        for c in range(RET_STEP_CHUNKS):
            rows = slice(c * L, (c + 1) * L)
            q2 = x_ref[rows, t * LANES:(t + 1) * LANES]
            k2 = x_ref[rows, B_WIDTH + t * LANES:B_WIDTH + (t + 1) * LANES]
            v2 = x_ref[rows, 2 * B_WIDTH + t * LANES:2 * B_WIDTH + (t + 1) * LANES]
            g2 = x_ref[rows, 3 * B_WIDTH + t * LANES:3 * B_WIDTH + (t + 1) * LANES]
            y = jnp.dot(q2, state.astype(BF16), preferred_element_type=F32) * xi_ref[:, cols]
            for hh in range(2):
                sc = qk_scr[(c * ntile + t) * 2 + hh] * decay_ref[2 * t + hh]
                o_h = jnp.dot(sc.astype(BF16), v2, preferred_element_type=F32)
                y = y + jnp.where(lane_head == hh, o_h, 0.0)
            kz = (k2.astype(F32) * zeta_ref[:, cols]).astype(BF16)
            kv = lax.dot_general(kz, v2, (((0,), (0,)), ((), ())), preferred_element_type=F32)
            state = cd_ref[t] * state + jnp.where(sq_row == sq_col, kv, 0.0)
            yn = _head_norm_pair(y, lane_head) * nw_ref[:, cols]
            o_ref[rows, cols] = (yn * _silu(g2.astype(F32))).astype(o_ref.dtype)
        state_scr[t] = state


def _ret(ret_in, decay, zeta, xi, cd, norm_w):
    s = ret_in.shape[0]
    L = RET_CHUNK
    rows = L * RET_STEP_CHUNKS
    ntile = B_WIDTH // LANES
    full = lambda shape: pl.BlockSpec(shape, lambda c: (0,) * len(shape))
    return pl.pallas_call(
        _ret_kernel,
        grid=(s // rows,),
        in_specs=[
            pl.BlockSpec((rows, 4 * B_WIDTH), lambda c: (c, 0)),
            full((B_HEADS, L, L)),
            full((L, B_WIDTH)),
            full((L, B_WIDTH)),
            full((ntile, LANES, LANES)),
            full((1, B_WIDTH)),
        ],
        out_specs=pl.BlockSpec((rows, B_WIDTH), lambda c: (c, 0)),
        out_shape=jax.ShapeDtypeStruct((s, B_WIDTH), BF16),
        scratch_shapes=[
            pltpu.VMEM((ntile, LANES, LANES), F32),
            pltpu.VMEM((RET_STEP_CHUNKS * ntile * 2, L, L), F32),
        ],
        compiler_params=pltpu.CompilerParams(
            dimension_semantics=("arbitrary",), vmem_limit_bytes=VMEM_LIMIT),
        name="ret",
    )(ret_in, decay, zeta, xi, cd, norm_w)


def _mlstm_kernel(x_ref, gates_ref, gb_ref, nw_ref, o_ref, st_scr, m_scr, qk_scr):
    step = pl.program_id(0)
    L = MLSTM_CHUNK
    ntile = C_WIDTH // LANES

    @pl.when(step == 0)
    def _():
        st_scr[...] = jnp.zeros_like(st_scr)
        m_scr[...] = jnp.zeros_like(m_scr)

    lane_head = lax.broadcasted_iota(jnp.int32, (L, LANES), 1) // HEAD_DIM
    lane_head1 = lane_head[0:1, :]
    sq_row = lax.broadcasted_iota(jnp.int32, (LANES, 2 * LANES), 0) // HEAD_DIM
    sq_col = (lax.broadcasted_iota(jnp.int32, (LANES, 2 * LANES), 1) % LANES) // HEAD_DIM
    same_head = sq_row == sq_col
    t_row = lax.broadcasted_iota(jnp.int32, (L, L), 0)
    t_col = lax.broadcasted_iota(jnp.int32, (L, L), 1)
    causal = t_row >= t_col
    lower = causal.astype(F32)
    ones_tile = jnp.ones((L, LANES), BF16)

    for c in range(MLSTM_STEP_CHUNKS):
        rows = slice(c * L, (c + 1) * L)
        for t in range(ntile):
            q2 = x_ref[rows, t * LANES:(t + 1) * LANES]
            k2 = x_ref[rows, C_WIDTH + t * LANES:C_WIDTH + (t + 1) * LANES]
            for hh in range(2):
                qm = jnp.where(lane_head == hh, q2, jnp.zeros_like(q2))
                qk_scr[(c * ntile + t) * 2 + hh] = lax.dot_general(
                    qm, k2, (((1,), (1,)), ((), ())), preferred_element_type=F32)

    local = []
    for c in range(MLSTM_STEP_CHUNKS):
        rows = slice(c * L, (c + 1) * L)
        pre = gates_ref[rows, :] + gb_ref[...]
        logsig = jnp.minimum(pre, 0.0) - jnp.log1p(jnp.exp(-jnp.abs(pre)))
        bsum = jnp.dot(lower, logsig, preferred_element_type=F32, precision=lax.Precision.HIGHEST)
        pre_t = pre.T
        bsum_t = bsum.T
        for t in range(ntile):
            k2 = x_ref[rows, C_WIDTH + t * LANES:C_WIDTH + (t + 1) * LANES]
            v2 = x_ref[rows, 2 * C_WIDTH + t * LANES:2 * C_WIDTH + (t + 1) * LANES]
            v_aug = jnp.concatenate([v2, ones_tile], axis=1)
            num_l = jnp.zeros((L, LANES), F32)
            den_l = jnp.zeros((L, LANES), F32)
            mloc_l = jnp.zeros((L, LANES), F32)
            b_l = jnp.zeros((L, LANES), F32)
            wg_l = jnp.zeros((L, LANES), F32)
            gmax_l = jnp.zeros((1, LANES), F32)
            blast_l = jnp.zeros((1, LANES), F32)
            for hh in range(2):
                h = 2 * t + hh
                in_h = lane_head == hh
                in_h1 = lane_head1 == hh
                i_col = pre[:, h:h + 1]
                b_col = bsum[:, C_HEADS + h:C_HEADS + h + 1]
                i_row = pre_t[h:h + 1, :]
                b_row = bsum_t[C_HEADS + h:C_HEADS + h + 1, :]
                d_log = jnp.where(causal, b_col - b_row + i_row, NEG_INF)
                m_loc = jnp.max(d_log, axis=-1, keepdims=True)
                sc = qk_scr[(c * ntile + t) * 2 + hh] * jnp.exp(d_log - m_loc)
                nd = jnp.dot(sc.astype(BF16), v_aug, preferred_element_type=F32)
                num_l = jnp.where(in_h, nd[:, 0:LANES], num_l)
                den_l = jnp.where(in_h, nd[:, LANES:2 * LANES], den_l)
                mloc_l = jnp.where(in_h, m_loc, mloc_l)
                b_l = jnp.where(in_h, b_col, b_l)
                b_last = b_col[L - 1:L, :]
                g_col = b_last - b_col + i_col
                g_max = jnp.max(g_col, axis=0, keepdims=True)
                wg_l = jnp.where(in_h, jnp.exp(g_col - g_max), wg_l)
                gmax_l = jnp.where(in_h1, g_max, gmax_l)
                blast_l = jnp.where(in_h1, b_last, blast_l)
            kw = (k2.astype(F32) * wg_l).astype(BF16)
            kv = lax.dot_general(kw, v_aug, (((0,), (0,)), ((), ())), preferred_element_type=F32)
            kv = jnp.where(same_head, kv, 0.0)
            local.append((num_l, den_l, mloc_l, b_l, gmax_l, blast_l, kv))

    for t in range(ntile):
        cols = slice(t * LANES, (t + 1) * LANES)
        state = st_scr[t]
        m_st = m_scr[t]
        for c in range(MLSTM_STEP_CHUNKS):
            rows = slice(c * L, (c + 1) * L)
            num_l, den_l, mloc_l, b_l, gmax_l, blast_l, kv = local[c * ntile + t]
            q2 = x_ref[rows, t * LANES:(t + 1) * LANES]
            o2 = x_ref[rows, 3 * C_WIDTH + t * LANES:3 * C_WIDTH + (t + 1) * LANES]
            g2 = x_ref[rows, 4 * C_WIDTH + t * LANES:4 * C_WIDTH + (t + 1) * LANES]
            qs = jnp.dot(q2, state.astype(BF16), preferred_element_type=F32)
            inter_log = b_l + m_st
            m_t = jnp.maximum(inter_log, mloc_l)
            r_intra = jnp.exp(mloc_l - m_t)
            w_inter = jnp.exp(inter_log - m_t)
            num = r_intra * num_l + w_inter * qs[:, 0:LANES]
            den = r_intra * den_l + w_inter * qs[:, LANES:2 * LANES]
            h_t = num / jnp.maximum(jnp.abs(den), jnp.exp(-m_t))
            m_new = jnp.maximum(blast_l + m_st, gmax_l)
            a = jnp.exp(blast_l + m_st - m_new)
            beta = jnp.exp(gmax_l - m_new)
            a2 = jnp.concatenate([a, a], axis=1)
            beta2 = jnp.concatenate([beta, beta], axis=1)
            state = a2 * state + beta2 * kv
            m_st = m_new
            y = h_t * jax.nn.sigmoid(o2.astype(F32))
            yn = _head_norm_pair(y, lane_head) * nw_ref[:, cols]
            o_ref[rows, cols] = (yn * _silu(g2.astype(F32))).astype(o_ref.dtype)
        st_scr[t] = state
        m_scr[t] = m_st


def _mlstm(mls_in, gates, gate_bias, norm_w):
    s = mls_in.shape[0]
    rows = MLSTM_CHUNK * MLSTM_STEP_CHUNKS
    npair = C_WIDTH // LANES
    full = lambda shape: pl.BlockSpec(shape, lambda c: (0,) * len(shape))
    return pl.pallas_call(
        _mlstm_kernel,
        grid=(s // rows,),
        in_specs=[
            pl.BlockSpec((rows, 5 * C_WIDTH), lambda c: (c, 0)),
            pl.BlockSpec((rows, LANES), lambda c: (c, 0)),
            full((1, LANES)),
            full((1, C_WIDTH)),
        ],
        out_specs=pl.BlockSpec((rows, C_WIDTH), lambda c: (c, 0)),
        out_shape=jax.ShapeDtypeStruct((s, C_WIDTH), BF16),
        scratch_shapes=[
            pltpu.VMEM((npair, LANES, 2 * LANES), F32),
            pltpu.VMEM((npair, 1, LANES), F32),
            pltpu.VMEM((MLSTM_STEP_CHUNKS * npair * 2, MLSTM_CHUNK, MLSTM_CHUNK), F32),
        ],
        compiler_params=pltpu.CompilerParams(
            dimension_semantics=("arbitrary",), vmem_limit_bytes=VMEM_LIMIT),
        name="mlstm",
    )(mls_in, gates, gate_bias, norm_w)


def _outproj_kernel(x_ref, ya_ref, yb_ref, yc_ref, wa_ref, wb_ref, wc_ref, fw_ref, o_ref, *, final_norm):
    acc = x_ref[...]
    acc = acc + jnp.dot(ya_ref[...], wa_ref[...], preferred_element_type=F32)
    acc = acc + jnp.dot(yb_ref[...], wb_ref[...], preferred_element_type=F32)
    acc = acc + jnp.dot(yc_ref[...], wc_ref[...], preferred_element_type=F32)
    if final_norm:
        ms = jnp.mean(acc * acc, axis=-1, keepdims=True)
        acc = (acc * lax.rsqrt(ms + RMS_EPS)) * fw_ref[...]
    o_ref[...] = acc


def _outproj(x, ya, yb, yc, wa, wb, wc, final_w, final_norm):
    s = x.shape[0]
    tm = OUT_ROWS
    full = lambda shape: pl.BlockSpec(shape, lambda i: (0,) * len(shape))
    rows = lambda width: pl.BlockSpec((tm, width), lambda i: (i, 0))
    return pl.pallas_call(
        functools.partial(_outproj_kernel, final_norm=final_norm),
        grid=(s // tm,),
        in_specs=[rows(D_MODEL), rows(A_WIDTH), rows(B_WIDTH), rows(C_WIDTH),
                  full((A_WIDTH, D_MODEL)), full((B_WIDTH, D_MODEL)), full((C_WIDTH, D_MODEL)),
                  full((1, D_MODEL))],
        out_specs=rows(D_MODEL),
        out_shape=jax.ShapeDtypeStruct((s, D_MODEL), F32),
        compiler_params=pltpu.CompilerParams(
            dimension_semantics=("arbitrary",), vmem_limit_bytes=VMEM_LIMIT),
        name="outproj",
    )(x, ya, yb, yc, wa, wb, wc, final_w)


def _rope_tables(s):
    half = HEAD_DIM // 2
    pos = jnp.arange(s, dtype=F32)
    theta = 1.0 / (ROPE_BASE ** jnp.linspace(0.0, 1.0, half, dtype=F32))
    ang = pos[:, None] * theta[None, :]
    cos, sin = jnp.cos(ang), jnp.sin(ang)
    cos_t = jnp.tile(cos, (1, LANES // half))
    sin_t = jnp.tile(jnp.concatenate([-sin, sin], axis=-1), (1, LANES // HEAD_DIM))
    return cos_t, sin_t


def _retention_tables():
    L = RET_CHUNK
    h = B_HEADS
    log_gamma = jnp.log(1.0 - jnp.exp2(-5.0 - jnp.arange(h, dtype=F32)))
    pos = jnp.arange(L, dtype=F32)
    diff = pos[:, None] - pos[None, :]
    decay = jnp.where(diff >= 0, jnp.exp(log_gamma[:, None, None] * jnp.maximum(diff, 0.0)), 0.0)
    zeta = jnp.exp(log_gamma[:, None] * (L - 1.0 - pos))
    xi = jnp.exp(log_gamma[:, None] * (pos + 1.0))
    chunk_decay = jnp.exp(log_gamma * L)
    per_lane = lambda t: jnp.repeat(t.T, HEAD_DIM, axis=1)
    cd_lane = jnp.repeat(chunk_decay, HEAD_DIM).reshape(B_WIDTH // LANES, LANES, 1)
    cd = jnp.broadcast_to(cd_lane, (B_WIDTH // LANES, LANES, LANES))
    return decay, per_lane(zeta), per_lane(xi), cd


def _alibi_slopes():
    return jnp.exp2(-8.0 * jnp.arange(1, A_HEADS + 1, dtype=F32) / A_HEADS) * LOG2E


def _split_bf16(t):
    parts, rest = [], t
    for _ in range(ALIBI_PARTS):
        piece = rest.astype(BF16).astype(F32)
        parts.append(piece)
        rest = rest - piece
    return jnp.stack(parts, axis=-1)


def kernel(x, norm_w, w_in, conv_w, conv_b, gate_bias, ret_norm_w, mlstm_norm_w, w_out, final_norm_w):
    b, s, d = x.shape
    assert b == 1 and d == D_MODEL and s % PROJ_ROWS == 0 and s % OUT_ROWS == 0
    assert s % (MLSTM_CHUNK * MLSTM_STEP_CHUNKS) == 0 and s % (RET_CHUNK * RET_STEP_CHUNKS) == 0
    depth = w_in.shape[0]
    cos_t, sin_t = _rope_tables(s)
    decay, zeta, xi, cd = _retention_tables()
    slopes = _alibi_slopes()
    alibi_parts = _split_bf16(slopes)

    w_main = w_in[:, :, :N_MAIN].astype(BF16)
    w_gate = jnp.pad(w_in[:, :, N_MAIN:], ((0, 0), (0, 0), (0, LANES - 2 * C_HEADS)))
    w_gate = w_gate.astype(BF16)
    w_out_b = w_out.astype(BF16)
    gb = jnp.pad(gate_bias.astype(F32), ((0, 0), (0, LANES - 2 * C_HEADS)))

    xs = x[0]
    for layer in range(depth):
        qT, ka, vT, ga, kmean, ret_in, mls_in, gates = _proj(
            xs, norm_w[layer][None, :], w_main[layer], w_gate[layer], cos_t, sin_t,
            conv_w[layer], conv_b[layer][None, :])
        ya = _moba(slopes, alibi_parts, qT, ka, vT, kmean.reshape(-1, A_WIDTH), ga)
        yb = _ret(ret_in, decay, zeta, xi, cd, ret_norm_w[layer][None, :])
        yc = _mlstm(mls_in, gates, gb[layer][None, :], mlstm_norm_w[layer][None, :])
        wo = w_out_b[layer]
        xs = _outproj(xs, ya, yb, yc, wo[:A_WIDTH], wo[A_WIDTH:A_WIDTH + B_WIDTH], wo[A_WIDTH + B_WIDTH:],
                      final_norm_w[None, :], final_norm=(layer == depth - 1))
    return xs[None]
```

```python
import functools

import numpy as np
import jax
import jax.numpy as jnp
from jax import lax
from jax.experimental import pallas as pl
from jax.experimental.pallas import tpu as pltpu

D_MODEL = 1024
HEAD_DIM = 64
A_HEADS, B_HEADS, C_HEADS = 8, 4, 4
A_WIDTH, B_WIDTH, C_WIDTH = A_HEADS * HEAD_DIM, B_HEADS * HEAD_DIM, C_HEADS * HEAD_DIM
MOBA_BLOCK = 256
MOBA_TOPK = 3
MOBA_GROUP = 4
MOBA_BIG_GROUP = 16
MOBA_HUGE_GROUP = 32
RET_CHUNK = 256
RET_STEP_CHUNKS = 4
MLSTM_CHUNK = 128
MLSTM_STEP_CHUNKS = 4
CONV_K = 4
RMS_EPS = 1e-6
HEAD_NORM_EPS = 1e-5
ROPE_BASE = 10000.0
N_MAIN = 4 * A_WIDTH + 4 * B_WIDTH + 5 * C_WIDTH
QK_SCALE = HEAD_DIM ** -0.5
V_ROWS = HEAD_DIM + 16
K_LANES = 128
ALIBI_PARTS = 3
LOG2E = 1.4426950408889634

LANES = 128
CONV_PAD = 8
PROJ_ROWS = 512
OUT_ROWS = 512
VMEM_LIMIT = 56 * 1024 * 1024

BF16 = jnp.bfloat16
F32 = jnp.float32
NEG_INF = float("-inf")


def _silu(t):
    return t * jax.nn.sigmoid(t)


def _swap_half_heads(t):
    lane = lax.broadcasted_iota(jnp.int32, t.shape, 1)
    first_half = (lane % HEAD_DIM) < (HEAD_DIM // 2)
    return jnp.where(first_half, pltpu.roll(t, LANES - HEAD_DIM // 2, 1), pltpu.roll(t, HEAD_DIM // 2, 1))


def _proj_kernel(x_ref, nw_ref, w_ref, wg_ref, cos_ref, sin_ref, cw_ref, cb_ref,
                 qT_ref, ka_ref, vT_ref, ga_ref, kmean_ref, ret_ref, mls_ref, gates_ref,
                 h_scr, ext_scr):
    i = pl.program_id(0)
    tm = x_ref.shape[0]
    nblk = tm // MOBA_BLOCK

    x = x_ref[...]
    ms = jnp.mean(x * x, axis=-1, keepdims=True)
    h_scr[...] = ((x * lax.rsqrt(ms + RMS_EPS)) * nw_ref[...]).astype(BF16)

    def seg(lo, width):
        return jnp.dot(h_scr[...], w_ref[:, lo:lo + width], preferred_element_type=F32)

    aq = seg(0, A_WIDTH) * (QK_SCALE * LOG2E)
    ak = seg(A_WIDTH, A_WIDTH)
    av = seg(2 * A_WIDTH, A_WIDTH)
    row_in_blk = (lax.broadcasted_iota(jnp.int32, (tm, LANES), 0) % MOBA_BLOCK).astype(F32)
    lane = lax.broadcasted_iota(jnp.int32, (tm, LANES), 1)
    for h in range(A_HEADS):
        own = (lane // HEAD_DIM) == (h % 2)
        is_feat = jnp.logical_and(jnp.logical_not(own), (lane % HEAD_DIM) < ALIBI_PARTS)
        pair_tile = ak[:, (h // 2) * LANES:(h // 2 + 1) * LANES]
        ka_ref[:, h * K_LANES:(h + 1) * K_LANES] = jnp.where(
            own, pair_tile, jnp.where(is_feat, row_in_blk, 0.0)).astype(BF16)
    ga_ref[...] = seg(3 * A_WIDTH, A_WIDTH).astype(BF16)
    for r in range(nblk):
        rows = slice(r * MOBA_BLOCK, (r + 1) * MOBA_BLOCK)
        qT_ref[r] = aq[rows].T.astype(BF16)
        avT = av[rows].T
        for h in range(A_HEADS):
            vT_ref[r, h * V_ROWS:h * V_ROWS + HEAD_DIM, :] = avT[h * HEAD_DIM:(h + 1) * HEAD_DIM].astype(BF16)
            vT_ref[r, h * V_ROWS + HEAD_DIM:(h + 1) * V_ROWS, :] = jnp.ones((V_ROWS - HEAD_DIM, MOBA_BLOCK), BF16)
        kmean_ref[r] = jnp.mean(ak[rows], axis=0, keepdims=True)

    b0 = 4 * A_WIDTH
    cos = cos_ref[...]
    sin = sin_ref[...]
    bqk = seg(b0, 2 * B_WIDTH)
    for t in range(2 * B_WIDTH // LANES):
        v = bqk[:, t * LANES:(t + 1) * LANES]
        rot = v * cos + _swap_half_heads(v) * sin
        if t >= B_WIDTH // LANES:
            rot = rot * QK_SCALE
        ret_ref[:, t * LANES:(t + 1) * LANES] = rot.astype(BF16)
    ret_ref[:, 2 * B_WIDTH:4 * B_WIDTH] = seg(b0 + 2 * B_WIDTH, 2 * B_WIDTH).astype(BF16)

    c0 = b0 + 4 * B_WIDTH

    @pl.when(i == 0)
    def _():
        ext_scr[0:CONV_PAD, :] = jnp.zeros((CONV_PAD, 2 * C_WIDTH), F32)

    ext_scr[CONV_PAD:CONV_PAD + tm, :] = seg(c0, 2 * C_WIDTH)
    y = jnp.zeros((tm, 2 * C_WIDTH), F32) + cb_ref[...]
    for tap in range(CONV_K):
        lo = CONV_PAD - (CONV_K - 1) + tap
        y = y + cw_ref[tap:tap + 1, :] * ext_scr[lo:lo + tm, :]
    ext_scr[0:CONV_PAD, :] = ext_scr[tm:tm + CONV_PAD, :]
    qk = _silu(y)
    mls_ref[:, 0:C_WIDTH] = qk[:, 0:C_WIDTH].astype(BF16)
    mls_ref[:, C_WIDTH:2 * C_WIDTH] = (qk[:, C_WIDTH:] * QK_SCALE).astype(BF16)
    mls_ref[:, 2 * C_WIDTH:5 * C_WIDTH] = seg(c0 + 2 * C_WIDTH, 3 * C_WIDTH).astype(BF16)

    gates_ref[...] = jnp.dot(h_scr[...], wg_ref[...], preferred_element_type=F32)


def _proj(x, norm_w, w_main, w_gate, cos_t, sin_t, conv_w, conv_b):
    s = x.shape[0]
    tm = PROJ_ROWS
    nblk = tm // MOBA_BLOCK
    nb = s // MOBA_BLOCK
    full = lambda shape: pl.BlockSpec(shape, lambda i: (0,) * len(shape))
    return pl.pallas_call(
        _proj_kernel,
        grid=(s // tm,),
        in_specs=[
            pl.BlockSpec((tm, D_MODEL), lambda i: (i, 0)),
            full((1, D_MODEL)),
            full((D_MODEL, N_MAIN)),
            full((D_MODEL, LANES)),
            pl.BlockSpec((tm, LANES), lambda i: (i, 0)),
            pl.BlockSpec((tm, LANES), lambda i: (i, 0)),
            full((CONV_K, 2 * C_WIDTH)),
            full((1, 2 * C_WIDTH)),
        ],
        out_specs=[
            pl.BlockSpec((nblk, A_WIDTH, MOBA_BLOCK), lambda i: (i, 0, 0)),
            pl.BlockSpec((tm, A_HEADS * K_LANES), lambda i: (i, 0)),
            pl.BlockSpec((nblk, A_HEADS * V_ROWS, MOBA_BLOCK), lambda i: (i, 0, 0)),
            pl.BlockSpec((tm, A_WIDTH), lambda i: (i, 0)),
            pl.BlockSpec((nblk, 1, A_WIDTH), lambda i: (i, 0, 0)),
            pl.BlockSpec((tm, 4 * B_WIDTH), lambda i: (i, 0)),
            pl.BlockSpec((tm, 5 * C_WIDTH), lambda i: (i, 0)),
            pl.BlockSpec((tm, LANES), lambda i: (i, 0)),
        ],
        out_shape=[
            jax.ShapeDtypeStruct((nb, A_WIDTH, MOBA_BLOCK), BF16),
            jax.ShapeDtypeStruct((s, A_HEADS * K_LANES), BF16),
            jax.ShapeDtypeStruct((nb, A_HEADS * V_ROWS, MOBA_BLOCK), BF16),
            jax.ShapeDtypeStruct((s, A_WIDTH), BF16),
            jax.ShapeDtypeStruct((nb, 1, A_WIDTH), F32),
            jax.ShapeDtypeStruct((s, 4 * B_WIDTH), BF16),
            jax.ShapeDtypeStruct((s, 5 * C_WIDTH), BF16),
            jax.ShapeDtypeStruct((s, LANES), F32),
        ],
        scratch_shapes=[
            pltpu.VMEM((tm, D_MODEL), BF16),
            pltpu.VMEM((tm + CONV_PAD, 2 * C_WIDTH), F32),
        ],
        compiler_params=pltpu.CompilerParams(
            dimension_semantics=("arbitrary",), vmem_limit_bytes=VMEM_LIMIT),
        name="proj",
    )(x, norm_w, w_main, w_gate, cos_t, sin_t, conv_w, conv_b)


def _moba_kernel(slopes_ref, alibi_ref, qT_ref, k_ref, vT_ref, kmean_ref, g_ref, o_ref,
                 qm_scr, sel_scr, m_scr, acc_scr, s_scr, mloc_scr, gate_scr, own_scr):
    p = pl.program_id(0)
    j = pl.program_id(1)
    blk = MOBA_BLOCK
    nb = kmean_ref.shape[0]

    qT2 = qT_ref[0]
    dim_row = lax.broadcasted_iota(jnp.int32, (LANES, blk), 0)
    blk_row = lax.broadcasted_iota(jnp.int32, (nb, blk), 0)
    key_off = lax.broadcasted_iota(jnp.int32, (blk, blk), 0)
    qry_off = lax.broadcasted_iota(jnp.int32, (blk, blk), 1)
    kmean = kmean_ref[...].astype(BF16)
    k_own = k_ref[pl.ds(pl.multiple_of(j * blk, blk), blk), :]
    vT_own = vT_ref[j]

    def make_group(size, first_block, limit):
        def stage(g):
            base = first_block + g * size
            for b in range(size):
                k_blk = k_ref[pl.ds(pl.multiple_of((base + b) * blk, blk), blk), :]
                for hh in range(2):
                    sc = jnp.dot(k_blk[:, hh * K_LANES:(hh + 1) * K_LANES], qm_scr[hh], preferred_element_type=F32)
                    s_scr[2 * b + hh] = sc
                    mloc_scr[2 * b + hh] = jnp.max(sc, axis=0, keepdims=True)

        def process(g):
            base = first_block + g * size
            parts = [[], []]
            for b in range(size):
                n = base + b
                vT_blk = vT_ref[n]
                dist = ((n - j) * blk).astype(F32)
                for hh in range(2):
                    m_loc = mloc_scr[2 * b + hh]
                    pm = jnp.exp2(s_scr[2 * b + hh] - m_loc)
                    o_loc = jnp.dot(vT_blk[hh * V_ROWS:(hh + 1) * V_ROWS, :], pm.astype(BF16),
                                    preferred_element_type=F32)
                    chosen = jnp.logical_and(sel_scr[hh, pl.ds(n, 1), :] > 0.0, n < limit)
                    cand = jnp.where(chosen, m_loc + slopes_ref[2 * p + hh] * dist, NEG_INF)
                    parts[hh].append((cand, o_loc))
            for hh in range(2):
                m_old = m_scr[hh]
                m_new = m_old
                for cand, _ in parts[hh]:
                    m_new = jnp.maximum(m_new, cand)
                acc = jnp.exp2(m_old - m_new) * acc_scr[hh]
                for cand, o_loc in parts[hh]:
                    acc = acc + jnp.exp2(cand - m_new) * o_loc
                m_scr[hh] = m_new
                acc_scr[hh] = acc

        def group(g, carry):
            stage(g)
            process(g)
            return carry
        return stage, process, group

    rest = j % MOBA_BIG_GROUP
    stage_short, process_short, short = make_group(MOBA_GROUP, 0, rest)

    for hh in range(2):
        qm = jnp.where((dim_row // HEAD_DIM) == hh, qT2, jnp.zeros_like(qT2))
        coef = jnp.zeros((LANES, blk), F32)
        for part in range(ALIBI_PARTS):
            coef = jnp.where(dim_row == (1 - hh) * HEAD_DIM + part, alibi_ref[2 * p + hh, part], coef)
        qa = jnp.where((dim_row // HEAD_DIM) == hh, qT2, coef.astype(BF16))
        qm_scr[hh] = qa
        gate_scr[hh] = jnp.dot(kmean, qm, preferred_element_type=F32)
        own_scr[hh] = jnp.dot(k_own[:, hh * K_LANES:(hh + 1) * K_LANES], qa, preferred_element_type=F32)
    stage_short(0)

    for hh in range(2):
        work = jnp.where(blk_row < j, gate_scr[hh], NEG_INF)
        sel = jnp.zeros((nb, blk), F32)
        for r in range(MOBA_TOPK):
            mx = jnp.max(work, axis=0, keepdims=True)
            idx = jnp.min(jnp.where(work == mx, blk_row, nb), axis=0, keepdims=True)
            pick = blk_row == idx
            sel = jnp.where(jnp.logical_and(pick, r < j), 1.0, sel)
            work = jnp.where(pick, NEG_INF, work)
        sel_scr[hh] = sel

        s = jnp.where(key_off <= qry_off, own_scr[hh], NEG_INF)
        m = jnp.max(s, axis=0, keepdims=True)
        pm = jnp.exp2(s - m)
        m_scr[hh] = m
        acc_scr[hh] = jnp.dot(vT_own[hh * V_ROWS:(hh + 1) * V_ROWS, :], pm.astype(BF16),
                              preferred_element_type=F32)

    process_short(0)
    lax.fori_loop(1, (rest + MOBA_GROUP - 1) // MOBA_GROUP, short, 0)
    n_big = ((j - rest) % MOBA_HUGE_GROUP) // MOBA_BIG_GROUP
    lax.fori_loop(0, n_big, make_group(MOBA_BIG_GROUP, rest, j)[2], 0)
    after_big = rest + n_big * MOBA_BIG_GROUP
    lax.fori_loop(0, (j - after_big) // MOBA_HUGE_GROUP, make_group(MOBA_HUGE_GROUP, after_big, j)[2], 0)

    oT = jnp.concatenate([acc_scr[hh, 0:HEAD_DIM, :] / acc_scr[hh, HEAD_DIM:HEAD_DIM + 1, :] for hh in range(2)],
                         axis=0)
    o_ref[...] = (oT.T * _silu(g_ref[...].astype(F32))).astype(o_ref.dtype)


def _moba(slopes, alibi_parts, qT, ka, vT, kmean, ga):
    s = ka.shape[0]
    nb = s // MOBA_BLOCK
    blk = MOBA_BLOCK
    npair = A_HEADS // 2
    return pl.pallas_call(
        _moba_kernel,
        grid=(npair, nb),
        in_specs=[
            pl.BlockSpec(memory_space=pltpu.SMEM),
            pl.BlockSpec(memory_space=pltpu.SMEM),
            pl.BlockSpec((1, LANES, blk), lambda p, j: (j, p, 0)),
            pl.BlockSpec((s, 2 * K_LANES), lambda p, j: (0, p)),
            pl.BlockSpec((nb, 2 * V_ROWS, blk), lambda p, j: (0, p, 0)),
            pl.BlockSpec((nb, LANES), lambda p, j: (0, p)),
            pl.BlockSpec((blk, LANES), lambda p, j: (j, p)),
        ],
        out_specs=pl.BlockSpec((blk, LANES), lambda p, j: (j, p)),
        out_shape=jax.ShapeDtypeStruct((s, A_WIDTH), BF16),
        scratch_shapes=[
            pltpu.VMEM((2, K_LANES, blk), BF16),
            pltpu.VMEM((2, nb, blk), F32),
            pltpu.VMEM((2, 1, blk), F32),
            pltpu.VMEM((2, V_ROWS, blk), F32),
            pltpu.VMEM((2 * MOBA_HUGE_GROUP, blk, blk), F32),
            pltpu.VMEM((2 * MOBA_HUGE_GROUP, 1, blk), F32),
            pltpu.VMEM((2, nb, blk), F32),
            pltpu.VMEM((2, blk, blk), F32),
        ],
        compiler_params=pltpu.CompilerParams(
            dimension_semantics=("arbitrary", "arbitrary"), vmem_limit_bytes=VMEM_LIMIT),
        name="moba",
    )(slopes, alibi_parts, qT, ka, vT, kmean, ga)


def _head_norm_pair(y, lane_head):
    mu = jnp.zeros_like(y)
    for hh in range(2):
        in_h = lane_head == hh
        mu_h = jnp.sum(jnp.where(in_h, y, 0.0), axis=-1, keepdims=True) * (1.0 / HEAD_DIM)
        mu = jnp.where(in_h, mu_h, mu)
    d = y - mu
    var = jnp.zeros_like(y)
    for hh in range(2):
        in_h = lane_head == hh
        var_h = jnp.sum(jnp.where(in_h, d * d, 0.0), axis=-1, keepdims=True) * (1.0 / HEAD_DIM)
        var = jnp.where(in_h, var_h, var)
    return d * lax.rsqrt(var + HEAD_NORM_EPS)


def _ret_kernel(x_ref, decay_ref, zeta_ref, xi_ref, cd_ref, nw_ref, o_ref, state_scr, qk_scr):
    step = pl.program_id(0)
    L = RET_CHUNK
    ntile = B_WIDTH // LANES

    @pl.when(step == 0)
    def _():
        state_scr[...] = jnp.zeros_like(state_scr)

    lane_head = lax.broadcasted_iota(jnp.int32, (L, LANES), 1) // HEAD_DIM
    sq_row = lax.broadcasted_iota(jnp.int32, (LANES, LANES), 0) // HEAD_DIM
    sq_col = lax.broadcasted_iota(jnp.int32, (LANES, LANES), 1) // HEAD_DIM

    for c in range(RET_STEP_CHUNKS):
        rows = slice(c * L, (c + 1) * L)
        for t in range(ntile):
            q2 = x_ref[rows, t * LANES:(t + 1) * LANES]
            k2 = x_ref[rows, B_WIDTH + t * LANES:B_WIDTH + (t + 1) * LANES]
            for hh in range(2):
                qm = jnp.where(lane_head == hh, q2, jnp.zeros_like(q2))
                qk_scr[(c * ntile + t) * 2 + hh] = lax.dot_general(
                    qm, k2, (((1,), (1,)), ((), ())), preferred_element_type=F32)

    for t in range(ntile):
        cols = slice(t * LANES, (t + 1) * LANES)
        state = state_scr[t]
        for c in range(RET_STEP_CHUNKS):
            rows = slice(c * L, (c + 1) * L)
            q2 = x_ref[rows, t * LANES:(t + 1) * LANES]
            k2 = x_ref[rows, B_WIDTH + t * LANES:B_WIDTH + (t + 1) * LANES]
            v2 = x_ref[rows, 2 * B_WIDTH + t * LANES:2 * B_WIDTH + (t + 1) * LANES]
            g2 = x_ref[rows, 3 * B_WIDTH + t * LANES:3 * B_WIDTH + (t + 1) * LANES]
            y = jnp.dot(q2, state.astype(BF16), preferred_element_type=F32) * xi_ref[:, cols]
            for hh in range(2):
                sc = qk_scr[(c * ntile + t) * 2 + hh] * decay_ref[2 * t + hh]
                o_h = jnp.dot(sc.astype(BF16), v2, preferred_element_type=F32)
                y = y + jnp.where(lane_head == hh, o_h, 0.0)
            kz = (k2.astype(F32) * zeta_ref[:, cols]).astype(BF16)
            kv = lax.dot_general(kz, v2, (((0,), (0,)), ((), ())), preferred_element_type=F32)
            state = cd_ref[t] * state + jnp.where(sq_row == sq_col, kv, 0.0)
            yn = _head_norm_pair(y, lane_head) * nw_ref[:, cols]
            o_ref[rows, cols] = (yn * _silu(g2.astype(F32))).astype(o_ref.dtype)
        state_scr[t] = state


def _ret(ret_in, decay, zeta, xi, cd, norm_w):
    s = ret_in.shape[0]
    L = RET_CHUNK
    rows = L * RET_STEP_CHUNKS
    ntile = B_WIDTH // LANES
    full = lambda shape: pl.BlockSpec(shape, lambda c: (0,) * len(shape))
    return pl.pallas_call(
        _ret_kernel,
        grid=(s // rows,),
        in_specs=[
            pl.BlockSpec((rows, 4 * B_WIDTH), lambda c: (c, 0)),
            full((B_HEADS, L, L)),
            full((L, B_WIDTH)),
            full((L, B_WIDTH)),
            full((ntile, LANES, LANES)),
            full((1, B_WIDTH)),
        ],
        out_specs=pl.BlockSpec((rows, B_WIDTH), lambda c: (c, 0)),
        out_shape=jax.ShapeDtypeStruct((s, B_WIDTH), BF16),
        scratch_shapes=[
            pltpu.VMEM((ntile, LANES, LANES), F32),
            pltpu.VMEM((RET_STEP_CHUNKS * ntile * 2, L, L), F32),
        ],
        compiler_params=pltpu.CompilerParams(
            dimension_semantics=("arbitrary",), vmem_limit_bytes=VMEM_LIMIT),
        name="ret",
    )(ret_in, decay, zeta, xi, cd, norm_w)


def _mlstm_kernel(x_ref, gates_ref, gb_ref, nw_ref, o_ref, st_scr, m_scr, qk_scr):
    step = pl.program_id(0)
    L = MLSTM_CHUNK
    ntile = C_WIDTH // LANES

    @pl.when(step == 0)
    def _():
        st_scr[...] = jnp.zeros_like(st_scr)
        m_scr[...] = jnp.zeros_like(m_scr)

    lane_head = lax.broadcasted_iota(jnp.int32, (L, LANES), 1) // HEAD_DIM
    lane_head1 = lane_head[0:1, :]
    sq_row = lax.broadcasted_iota(jnp.int32, (LANES, 2 * LANES), 0) // HEAD_DIM
    sq_col = (lax.broadcasted_iota(jnp.int32, (LANES, 2 * LANES), 1) % LANES) // HEAD_DIM
    same_head = sq_row == sq_col
    t_row = lax.broadcasted_iota(jnp.int32, (L, L), 0)
    t_col = lax.broadcasted_iota(jnp.int32, (L, L), 1)
    causal = t_row >= t_col
    lower = causal.astype(F32)
    ones_tile = jnp.ones((L, LANES), BF16)

    for c in range(MLSTM_STEP_CHUNKS):
        rows = slice(c * L, (c + 1) * L)
        for t in range(ntile):
            q2 = x_ref[rows, t * LANES:(t + 1) * LANES]
            k2 = x_ref[rows, C_WIDTH + t * LANES:C_WIDTH + (t + 1) * LANES]
            for hh in range(2):
                qm = jnp.where(lane_head == hh, q2, jnp.zeros_like(q2))
                qk_scr[(c * ntile + t) * 2 + hh] = lax.dot_general(
                    qm, k2, (((1,), (1,)), ((), ())), preferred_element_type=F32)

    local = []
    for c in range(MLSTM_STEP_CHUNKS):
        rows = slice(c * L, (c + 1) * L)
        pre = gates_ref[rows, :] + gb_ref[...]
        logsig = jnp.minimum(pre, 0.0) - jnp.log1p(jnp.exp(-jnp.abs(pre)))
        bsum = jnp.dot(lower, logsig, preferred_element_type=F32, precision=lax.Precision.HIGHEST)
        pre_t = pre.T
        bsum_t = bsum.T
        for t in range(ntile):
            k2 = x_ref[rows, C_WIDTH + t * LANES:C_WIDTH + (t + 1) * LANES]
            v2 = x_ref[rows, 2 * C_WIDTH + t * LANES:2 * C_WIDTH + (t + 1) * LANES]
            v_aug = jnp.concatenate([v2, ones_tile], axis=1)
            num_l = jnp.zeros((L, LANES), F32)
            den_l = jnp.zeros((L, LANES), F32)
            mloc_l = jnp.zeros((L, LANES), F32)
            b_l = jnp.zeros((L, LANES), F32)
            wg_l = jnp.zeros((L, LANES), F32)
            gmax_l = jnp.zeros((1, LANES), F32)
            blast_l = jnp.zeros((1, LANES), F32)
            for hh in range(2):
                h = 2 * t + hh
                in_h = lane_head == hh
                in_h1 = lane_head1 == hh
                i_col = pre[:, h:h + 1]
                b_col = bsum[:, C_HEADS + h:C_HEADS + h + 1]
                i_row = pre_t[h:h + 1, :]
                b_row = bsum_t[C_HEADS + h:C_HEADS + h + 1, :]
                d_log = jnp.where(causal, b_col - b_row + i_row, NEG_INF)
                m_loc = jnp.max(d_log, axis=-1, keepdims=True)
                sc = qk_scr[(c * ntile + t) * 2 + hh] * jnp.exp(d_log - m_loc)
                nd = jnp.dot(sc.astype(BF16), v_aug, preferred_element_type=F32)
                num_l = jnp.where(in_h, nd[:, 0:LANES], num_l)
                den_l = jnp.where(in_h, nd[:, LANES:2 * LANES], den_l)
                mloc_l = jnp.where(in_h, m_loc, mloc_l)
                b_l = jnp.where(in_h, b_col, b_l)
                b_last = b_col[L - 1:L, :]
                g_col = b_last - b_col + i_col
                g_max = jnp.max(g_col, axis=0, keepdims=True)
                wg_l = jnp.where(in_h, jnp.exp(g_col - g_max), wg_l)
                gmax_l = jnp.where(in_h1, g_max, gmax_l)
                blast_l = jnp.where(in_h1, b_last, blast_l)
            kw = (k2.astype(F32) * wg_l).astype(BF16)
            kv = lax.dot_general(kw, v_aug, (((0,), (0,)), ((), ())), preferred_element_type=F32)
            kv = jnp.where(same_head, kv, 0.0)
            local.append((num_l, den_l, mloc_l, b_l, gmax_l, blast_l, kv))

    for t in range(ntile):
        cols = slice(t * LANES, (t + 1) * LANES)
        state = st_scr[t]
        m_st = m_scr[t]
        for c in range(MLSTM_STEP_CHUNKS):
            rows = slice(c * L, (c + 1) * L)
            num_l, den_l, mloc_l, b_l, gmax_l, blast_l, kv = local[c * ntile + t]
            q2 = x_ref[rows, t * LANES:(t + 1) * LANES]
            o2 = x_ref[rows, 3 * C_WIDTH + t * LANES:3 * C_WIDTH + (t + 1) * LANES]
            g2 = x_ref[rows, 4 * C_WIDTH + t * LANES:4 * C_WIDTH + (t + 1) * LANES]
            qs = jnp.dot(q2, state.astype(BF16), preferred_element_type=F32)
            inter_log = b_l + m_st
            m_t = jnp.maximum(inter_log, mloc_l)
            r_intra = jnp.exp(mloc_l - m_t)
            w_inter = jnp.exp(inter_log - m_t)
            num = r_intra * num_l + w_inter * qs[:, 0:LANES]
            den = r_intra * den_l + w_inter * qs[:, LANES:2 * LANES]
            h_t = num / jnp.maximum(jnp.abs(den), jnp.exp(-m_t))
            m_new = jnp.maximum(blast_l + m_st, gmax_l)
            a = jnp.exp(blast_l + m_st - m_new)
            beta = jnp.exp(gmax_l - m_new)
            a2 = jnp.concatenate([a, a], axis=1)
            beta2 = jnp.concatenate([beta, beta], axis=1)
            state = a2 * state + beta2 * kv
            m_st = m_new
            y = h_t * jax.nn.sigmoid(o2.astype(F32))
            yn = _head_norm_pair(y, lane_head) * nw_ref[:, cols]
            o_ref[rows, cols] = (yn * _silu(g2.astype(F32))).astype(o_ref.dtype)
        st_scr[t] = state
        m_scr[t] = m_st


def _mlstm(mls_in, gates, gate_bias, norm_w):
    s = mls_in.shape[0]
    rows = MLSTM_CHUNK * MLSTM_STEP_CHUNKS
    npair = C_WIDTH // LANES
    full = lambda shape: pl.BlockSpec(shape, lambda c: (0,) * len(shape))
    return pl.pallas_call(
        _mlstm_kernel,
        grid=(s // rows,),
        in_specs=[
            pl.BlockSpec((rows, 5 * C_WIDTH), lambda c: (c, 0)),
            pl.BlockSpec((rows, LANES), lambda c: (c, 0)),
            full((1, LANES)),
            full((1, C_WIDTH)),
        ],
        out_specs=pl.BlockSpec((rows, C_WIDTH), lambda c: (c, 0)),
        out_shape=jax.ShapeDtypeStruct((s, C_WIDTH), BF16),
        scratch_shapes=[
            pltpu.VMEM((npair, LANES, 2 * LANES), F32),
            pltpu.VMEM((npair, 1, LANES), F32),
            pltpu.VMEM((MLSTM_STEP_CHUNKS * npair * 2, MLSTM_CHUNK, MLSTM_CHUNK), F32),
        ],
        compiler_params=pltpu.CompilerParams(
            dimension_semantics=("arbitrary",), vmem_limit_bytes=VMEM_LIMIT),
        name="mlstm",
    )(mls_in, gates, gate_bias, norm_w)


def _outproj_kernel(x_ref, ya_ref, yb_ref, yc_ref, wa_ref, wb_ref, wc_ref, fw_ref, o_ref, *, final_norm):
    acc = x_ref[...]
    acc = acc + jnp.dot(ya_ref[...], wa_ref[...], preferred_element_type=F32)
    acc = acc + jnp.dot(yb_ref[...], wb_ref[...], preferred_element_type=F32)
    acc = acc + jnp.dot(yc_ref[...], wc_ref[...], preferred_element_type=F32)
    if final_norm:
        ms = jnp.mean(acc * acc, axis=-1, keepdims=True)
        acc = (acc * lax.rsqrt(ms + RMS_EPS)) * fw_ref[...]
    o_ref[...] = acc


def _outproj(x, ya, yb, yc, wa, wb, wc, final_w, final_norm):
    s = x.shape[0]
    tm = OUT_ROWS
    full = lambda shape: pl.BlockSpec(shape, lambda i: (0,) * len(shape))
    rows = lambda width: pl.BlockSpec((tm, width), lambda i: (i, 0))
    return pl.pallas_call(
        functools.partial(_outproj_kernel, final_norm=final_norm),
        grid=(s // tm,),
        in_specs=[rows(D_MODEL), rows(A_WIDTH), rows(B_WIDTH), rows(C_WIDTH),
                  full((A_WIDTH, D_MODEL)), full((B_WIDTH, D_MODEL)), full((C_WIDTH, D_MODEL)),
                  full((1, D_MODEL))],
        out_specs=rows(D_MODEL),
        out_shape=jax.ShapeDtypeStruct((s, D_MODEL), F32),
        compiler_params=pltpu.CompilerParams(
            dimension_semantics=("arbitrary",), vmem_limit_bytes=VMEM_LIMIT),
        name="outproj",
    )(x, ya, yb, yc, wa, wb, wc, final_w)


def _rope_tables(s):
    half = HEAD_DIM // 2
    pos = jnp.arange(s, dtype=F32)
    theta = 1.0 / (ROPE_BASE ** jnp.linspace(0.0, 1.0, half, dtype=F32))
    ang = pos[:, None] * theta[None, :]
    cos, sin = jnp.cos(ang), jnp.sin(ang)
    cos_t = jnp.tile(cos, (1, LANES // half))
    sin_t = jnp.tile(jnp.concatenate([-sin, sin], axis=-1), (1, LANES // HEAD_DIM))
    return cos_t, sin_t


def _retention_tables():
    L = RET_CHUNK
    h = B_HEADS
    log_gamma = jnp.log(1.0 - jnp.exp2(-5.0 - jnp.arange(h, dtype=F32)))
    pos = jnp.arange(L, dtype=F32)
    diff = pos[:, None] - pos[None, :]
    decay = jnp.where(diff >= 0, jnp.exp(log_gamma[:, None, None] * jnp.maximum(diff, 0.0)), 0.0)
    zeta = jnp.exp(log_gamma[:, None] * (L - 1.0 - pos))
    xi = jnp.exp(log_gamma[:, None] * (pos + 1.0))
    chunk_decay = jnp.exp(log_gamma * L)
    per_lane = lambda t: jnp.repeat(t.T, HEAD_DIM, axis=1)
    cd_lane = jnp.repeat(chunk_decay, HEAD_DIM).reshape(B_WIDTH // LANES, LANES, 1)
    cd = jnp.broadcast_to(cd_lane, (B_WIDTH // LANES, LANES, LANES))
    return decay, per_lane(zeta), per_lane(xi), cd


def _alibi_slopes():
    return jnp.exp2(-8.0 * jnp.arange(1, A_HEADS + 1, dtype=F32) / A_HEADS) * LOG2E


def _split_bf16(t):
    parts, rest = [], t
    for _ in range(ALIBI_PARTS):
        piece = rest.astype(BF16).astype(F32)
        parts.append(piece)
        rest = rest - piece
    return jnp.stack(parts, axis=-1)


def kernel(x, norm_w, w_in, conv_w, conv_b, gate_bias, ret_norm_w, mlstm_norm_w, w_out, final_norm_w):
    b, s, d = x.shape
    assert b == 1 and d == D_MODEL and s % PROJ_ROWS == 0 and s % OUT_ROWS == 0
    assert s % (MLSTM_CHUNK * MLSTM_STEP_CHUNKS) == 0 and s % (RET_CHUNK * RET_STEP_CHUNKS) == 0
    depth = w_in.shape[0]
    cos_t, sin_t = _rope_tables(s)
    decay, zeta, xi, cd = _retention_tables()
    slopes = _alibi_slopes()
    alibi_parts = _split_bf16(slopes)

    w_main = w_in[:, :, :N_MAIN].astype(BF16)
    w_gate = jnp.pad(w_in[:, :, N_MAIN:], ((0, 0), (0, 0), (0, LANES - 2 * C_HEADS)))
    w_gate = w_gate.astype(BF16)
    w_out_b = w_out.astype(BF16)
    gb = jnp.pad(gate_bias.astype(F32), ((0, 0), (0, LANES - 2 * C_HEADS)))

    xs = x[0]
    for layer in range(depth):
        qT, ka, vT, ga, kmean, ret_in, mls_in, gates = _proj(
            xs, norm_w[layer][None, :], w_main[layer], w_gate[layer], cos_t, sin_t,
            conv_w[layer], conv_b[layer][None, :])
        ya = _moba(slopes, alibi_parts, qT, ka, vT, kmean.reshape(-1, A_WIDTH), ga)
        yb = _ret(ret_in, decay, zeta, xi, cd, ret_norm_w[layer][None, :])
        yc = _mlstm(mls_in, gates, gb[layer][None, :], mlstm_norm_w[layer][None, :])
        wo = w_out_b[layer]
        xs = _outproj(xs, ya, yb, yc, wo[:A_WIDTH], wo[A_WIDTH:A_WIDTH + B_WIDTH], wo[A_WIDTH + B_WIDTH:],
                      final_norm_w[None, :], final_norm=(layer == depth - 1))
    return xs[None]
```
